```python
import math
import jax, jax.numpy as jnp
from jax import lax
import numpy as np

D_MODEL = 2048
BATCH = 16
SEQ = 256
DEPTH = 4
DEC_BATCH = 8
DEC_SEQ = 4096
PAST_LEN = 256

GRID_W = 64
HY_C = 512
HY_ORDER = 2
SHORT_K = 3
FILT_EMB = 33
FILT_HID = 64
HY_FAST_DECAY = 0.3
HY_SLOW_DECAY = 1.5
HY_TARGET = 1e-2
CM_HEADS = 4
CM_HD = 128
CM_C = CM_HEADS * CM_HD
CHUNK = 128
N_HEADS = 8
N_KV_HEADS = 2
HEAD_DIM = 128
GQA = N_HEADS // N_KV_HEADS
ATT_C = N_HEADS * HEAD_DIM
Q_BLOCK = 128
ROPE_THETA = 10000.0
ROPE_AXIS = HEAD_DIM // 2
D_MIX = HY_C + CM_C + ATT_C
SPLITS = ((HY_ORDER + 1) * HY_C, HY_C, CM_C, CM_C, CM_C, ATT_C, N_KV_HEADS * HEAD_DIM, N_KV_HEADS * HEAD_DIM, ATT_C)
D_IN = sum(SPLITS)
EPS = 1e-6
DN_ALPHA = (2 * DEPTH) ** 0.25
DN_BETA = (8 * DEPTH) ** -0.25

kernel_name = "hymba_hyena_chunkmlp_gqa_dit_step"

F32 = jnp.float32


def _layernorm(x, g, b):
    xf = x.astype(F32)
    mu = jnp.mean(xf, axis=-1, keepdims=True)
    var = jnp.mean(jnp.square(xf - mu), axis=-1, keepdims=True)
    return ((xf - mu) * lax.rsqrt(var + EPS) * g.astype(F32) + b.astype(F32)).astype(x.dtype)


def _rmsnorm(x, g):
    xf = x.astype(F32)
    return (xf * lax.rsqrt(jnp.mean(xf * xf, axis=-1, keepdims=True) + EPS) * g.astype(F32)).astype(x.dtype)


def _short_conv(z, w, b):
    L = z.shape[1]
    pad = SHORT_K // 2
    zp = jnp.pad(z, ((0, 0), (pad, SHORT_K - 1 - pad), (0, 0)))
    out = zp[:, 0:L] * w[0]
    for k in range(1, SHORT_K):
        out = out + zp[:, k:k + L] * w[k]
    return out + b


def _implicit_filters(L, w1, b1, w2, b2, w3, freq):
    pos = jnp.arange(L, dtype=F32)
    t = pos / max(L - 1, 1)
    bands = (FILT_EMB - 1) // 2
    fb = jnp.linspace(1e-4, bands - 1, bands, dtype=F32)
    ang = (2.0 * math.pi / L) * pos[:, None] * fb[None, :]
    feats = jnp.concatenate([t[:, None], jnp.cos(ang), -jnp.sin(ang)], axis=-1)
    fr = freq.astype(F32)
    h = jnp.sin(fr * (feats @ w1.astype(F32) + b1.astype(F32)))
    h = jnp.sin(fr * (h @ w2.astype(F32) + b2.astype(F32)))
    h = (h @ w3.astype(F32)).reshape(L, HY_ORDER, HY_C)
    dist = jnp.abs(pos - (L // 2)) / (L / 2.0)
    deltas = jnp.abs(jnp.linspace(math.log(HY_TARGET) / HY_SLOW_DECAY,
                                  math.log(HY_TARGET) / HY_FAST_DECAY, HY_C, dtype=F32))
    window = jnp.exp(-dist[:, None] * deltas[None, :])
    return h * window[:, None, :]


def _centred_fftconv(z, h, skip):
    L = z.shape[1]
    n = 2 * L
    zf = jnp.fft.rfft(z, n=n, axis=1)
    hf = jnp.fft.rfft(h, n=n, axis=0)
    y = jnp.fft.irfft(zf * hf[None], n=n, axis=1)[:, L // 2:L // 2 + L]
    return y + skip * z


def _hyena(zin, conv_w, conv_b, w1, b1, w2, b2, w3, freq, skip):
    L = zin.shape[1]
    z = _short_conv(zin, conv_w, conv_b).astype(F32)
    h = _implicit_filters(L, w1, b1, w2, b2, w3, freq)
    y = z[..., :HY_C]
    for n in range(HY_ORDER):
        gate_n = z[..., (n + 1) * HY_C:(n + 2) * HY_C]
        y = gate_n * _centred_fftconv(y, h[:, n], skip[n].astype(F32))
    return y.astype(zin.dtype)


def _chunk_mlp(u, v, ln_g, ln_b, w_s, b_s):
    B, L, _ = v.shape
    v = _layernorm(v, ln_g, ln_b)
    vc = v.reshape(B, L // CHUNK, CHUNK, CM_HEADS, CM_HD)
    s = jnp.einsum('hpq,bnqhc->bnphc', w_s, vc) + b_s.T[None, None, :, :, None]
    return u * s.reshape(B, L, CM_C)


def _rope_axis(x, ang):
    F = ang.shape[-1]
    cos = jnp.cos(ang)[None, :, None, :].astype(x.dtype)
    sin = jnp.sin(ang)[None, :, None, :].astype(x.dtype)
    x1, x2 = x[..., :F], x[..., F:]
    return jnp.concatenate([x1 * cos - x2 * sin, x1 * sin + x2 * cos], axis=-1)


def _rope_2d(x, row_ang, col_ang):
    return jnp.concatenate([_rope_axis(x[..., :ROPE_AXIS], row_ang),
                            _rope_axis(x[..., ROPE_AXIS:], col_ang)], axis=-1)


def _block_attention(q, k, v):
    B, L, H, D = q.shape
    nb = L // Q_BLOCK
    qb = q.reshape(B, nb, Q_BLOCK, N_KV_HEADS, GQA, D).transpose(1, 0, 2, 3, 4, 5)
    scale = 1.0 / math.sqrt(D)

    def one_block(qblk):
        s = jnp.einsum('bqkgd,bskd->bkgqs', qblk, k).astype(F32) * scale
        p = jax.nn.softmax(s, axis=-1).astype(v.dtype)
        return jnp.einsum('bkgqs,bskd->bqkgd', p, v)

    o = lax.map(one_block, qb)
    return o.transpose(1, 0, 2, 3, 4, 5).reshape(B, L, H * D)


def _mixer_layer(x, shift, scale, gate, w_in, hy_conv_w, hy_conv_b, hy_f_w1, hy_f_b1, hy_f_w2, hy_f_b2,
                 hy_f_w3, hy_f_freq, hy_skip, cm_ln_g, cm_ln_b, cm_w_s, cm_b_s, q_norm_g, k_norm_g,
                 w_out, ln_g, ln_b, ctx_k, ctx_v, rope):
    B, L, _ = x.shape
    h = x * (1 + scale) + shift
    p = h @ w_in
    idx = np.cumsum(SPLITS)[:-1].tolist()
    hy_z, hy_g, cm_u, cm_v, cm_g, q, k, v, at_g = jnp.split(p, idx, axis=-1)
    y_hy = _hyena(hy_z, hy_conv_w, hy_conv_b, hy_f_w1, hy_f_b1, hy_f_w2, hy_f_b2, hy_f_w3, hy_f_freq,
                  hy_skip) * jax.nn.silu(hy_g)
    y_cm = _chunk_mlp(cm_u, cm_v, cm_ln_g, cm_ln_b, cm_w_s, cm_b_s) * jax.nn.silu(cm_g)
    q = _rmsnorm(q.reshape(B, L, N_HEADS, HEAD_DIM), q_norm_g)
    k = _rmsnorm(k.reshape(B, L, N_KV_HEADS, HEAD_DIM), k_norm_g)
    v = v.reshape(B, L, N_KV_HEADS, HEAD_DIM)
    if ctx_k is None:
        k_all, v_all = k, v
    else:
        row_ang, col_ang = rope
        q = _rope_2d(q, row_ang, col_ang)
        k_all = jnp.concatenate([_rope_2d(k, row_ang, col_ang), ctx_k], axis=1)
        v_all = jnp.concatenate([v, ctx_v], axis=1)
    y_at = _block_attention(q, k_all, v_all) * jax.nn.silu(at_g)
    out = jnp.concatenate([y_hy, y_cm, y_at], axis=-1) @ w_out
    x_new = _layernorm(DN_ALPHA * x + gate * out, ln_g, ln_b)
    return x_new, k, v


def setup_inputs(seed: int = 0) -> dict:
    key = jax.random.key(seed)
    ks = jax.random.split(key, 32)

    def nrm(k, shape, s=1.0):
        return jax.random.normal(k, shape, dtype=F32) * s

    return {
        "x_prompt": nrm(ks[0], (BATCH, SEQ, D_MODEL)),
        "x_sample": nrm(ks[1], (DEC_BATCH, DEC_SEQ, D_MODEL)),
        "cache_k": nrm(ks[2], (DEC_BATCH, DEPTH, PAST_LEN, N_KV_HEADS, HEAD_DIM)),
        "cache_v": nrm(ks[3], (DEC_BATCH, DEPTH, PAST_LEN, N_KV_HEADS, HEAD_DIM)),
        "c": nrm(ks[4], (DEC_BATCH, D_MODEL)),
        "c_ctx": nrm(ks[5], (D_MODEL,)),
        "w_mod": nrm(ks[6], (DEPTH, D_MODEL, 3 * D_MODEL), 0.5 * D_MODEL ** -0.5),
        "b_mod": nrm(ks[7], (DEPTH, 3 * D_MODEL), 0.01),
        "w_in": nrm(ks[8], (DEPTH, D_MODEL, D_IN), D_MODEL ** -0.5),
        "hy_conv_w": nrm(ks[9], (DEPTH, SHORT_K, (HY_ORDER + 1) * HY_C), SHORT_K ** -0.5),
        "hy_conv_b": nrm(ks[10], (DEPTH, (HY_ORDER + 1) * HY_C), 0.01),
        "hy_f_w1": nrm(ks[11], (DEPTH, FILT_EMB, FILT_HID), FILT_EMB ** -0.5),
        "hy_f_b1": nrm(ks[12], (DEPTH, FILT_HID), 0.1),
        "hy_f_w2": nrm(ks[13], (DEPTH, FILT_HID, FILT_HID), FILT_HID ** -0.5),
        "hy_f_b2": nrm(ks[14], (DEPTH, FILT_HID), 0.1),
        "hy_f_w3": nrm(ks[15], (DEPTH, FILT_HID, HY_ORDER * HY_C), 0.1 * FILT_HID ** -0.5),
        "hy_f_freq": 1.0 + nrm(ks[16], (DEPTH, FILT_HID), 0.01),
        "hy_skip": nrm(ks[17], (DEPTH, HY_ORDER, HY_C)),
        "cm_ln_g": 1.0 + nrm(ks[18], (DEPTH, CM_C), 0.01),
        "cm_ln_b": nrm(ks[19], (DEPTH, CM_C), 0.01),
        "cm_w_s": nrm(ks[20], (DEPTH, CM_HEADS, CHUNK, CHUNK), CHUNK ** -0.5),
        "cm_b_s": 1.0 + nrm(ks[21], (DEPTH, CM_HEADS, CHUNK), 0.01),
        "q_norm_g": 1.0 + nrm(ks[22], (DEPTH, HEAD_DIM), 0.01),
        "k_norm_g": 1.0 + nrm(ks[23], (DEPTH, HEAD_DIM), 0.01),
        "w_out": nrm(ks[24], (DEPTH, D_MIX, D_MODEL), DN_BETA * D_MIX ** -0.5),
        "ln_g": 1.0 + nrm(ks[25], (DEPTH, D_MODEL), 0.01),
        "ln_b": nrm(ks[26], (DEPTH, D_MODEL), 0.01),
    }


def reference(x_prompt, x_sample, cache_k, cache_v, c, c_ctx, w_mod, b_mod, w_in, hy_conv_w, hy_conv_b,
              hy_f_w1, hy_f_b1, hy_f_w2, hy_f_b2, hy_f_w3, hy_f_freq, hy_skip, cm_ln_g, cm_ln_b, cm_w_s,
              cm_b_s, q_norm_g, k_norm_g, w_out, ln_g, ln_b):
    L_lat = x_sample.shape[1]
    rows = L_lat // GRID_W
    row = jnp.repeat(jnp.arange(rows, dtype=F32), GRID_W)
    col = jnp.tile(jnp.arange(GRID_W, dtype=F32), rows)
    inv = ROPE_THETA ** (-jnp.arange(0, ROPE_AXIS, 2, dtype=F32) / ROPE_AXIS)
    row_ang = row[:, None] * inv[None, :]
    col_ang = col[:, None] * inv[None, :]

    xp = x_prompt
    xs = x_sample
    new_k = []
    new_v = []
    for l in range(DEPTH):
        lp = (w_in[l], hy_conv_w[l], hy_conv_b[l], hy_f_w1[l], hy_f_b1[l], hy_f_w2[l], hy_f_b2[l],
              hy_f_w3[l], hy_f_freq[l], hy_skip[l], cm_ln_g[l], cm_ln_b[l], cm_w_s[l], cm_b_s[l],
              q_norm_g[l], k_norm_g[l], w_out[l], ln_g[l], ln_b[l])
        m_ctx = jax.nn.silu(c_ctx) @ w_mod[l] + b_mod[l]
        sh, sc, gt = jnp.split(m_ctx, 3, axis=-1)
        xp, k_l, v_l = _mixer_layer(xp, sh, sc, gt, *lp, ctx_k=None, ctx_v=None, rope=None)
        new_k.append(k_l)
        new_v.append(v_l)
        m_lat = (jax.nn.silu(c) @ w_mod[l] + b_mod[l])[:, None, :]
        sh, sc, gt = jnp.split(m_lat, 3, axis=-1)
        xs, _, _ = _mixer_layer(xs, sh, sc, gt, *lp, ctx_k=cache_k[:, l], ctx_v=cache_v[:, l],
                                rope=(row_ang, col_ang))
    new_cache_k = jnp.stack(new_k, axis=1)
    new_cache_v = jnp.stack(new_v, axis=1)
    return (xp, xs, new_cache_k, new_cache_v)
```

```python
import functools
import math

import numpy as np
import jax
import jax.numpy as jnp
from jax import lax
from jax.experimental import pallas as pl
from jax.experimental.pallas import tpu as pltpu

F32 = jnp.float32
BF16 = jnp.bfloat16

D_MODEL = 2048
DEPTH = 4
GRID_W = 64
HY_C = 512
HY_ORDER = 2
SHORT_K = 3
FILT_EMB = 33
FILT_HID = 64
HY_FAST_DECAY = 0.3
HY_SLOW_DECAY = 1.5
HY_TARGET = 1e-2
CM_HEADS = 4
CM_HD = 128
CM_C = CM_HEADS * CM_HD
CHUNK = 128
N_HEADS = 8
N_KV_HEADS = 2
HEAD_DIM = 128
GQA = N_HEADS // N_KV_HEADS
ATT_C = N_HEADS * HEAD_DIM
KV_C = N_KV_HEADS * HEAD_DIM
ROPE_THETA = 10000.0
ROPE_AXIS = HEAD_DIM // 2
D_MIX = HY_C + CM_C + ATT_C
EPS = 1e-6
DN_ALPHA = (2 * DEPTH) ** 0.25

OFF_HYZ = 0
OFF_HYG = (HY_ORDER + 1) * HY_C
OFF_CMU = OFF_HYG + HY_C
OFF_CMV = OFF_CMU + CM_C
OFF_CMG = OFF_CMV + CM_C
OFF_Q = OFF_CMG + CM_C
OFF_K = OFF_Q + ATT_C
OFF_V = OFF_K + KV_C
OFF_ATG = OFF_V + KV_C
D_IN = OFF_ATG + ATT_C

LANES = 128
VMEM_LIMIT_BYTES = 56 * 1024 * 1024

DFT_INNER = 128
MOD_ROWS = 16


def _params(*semantics):
    return pltpu.CompilerParams(dimension_semantics=semantics, vmem_limit_bytes=VMEM_LIMIT_BYTES)


def _silu(x):
    return x * jax.nn.sigmoid(x)


def _dot(a, b):
    return jnp.dot(a, b, preferred_element_type=F32)


def _dot_exact(a, b):
    return jnp.dot(a, b, preferred_element_type=F32, precision=lax.Precision.HIGHEST)


def _mod_body(c_ref, w_ref, b_ref, o_ref):
    s = _silu(c_ref[...]).astype(BF16)
    o_ref[...] = _dot(s, w_ref[...].astype(BF16)) + b_ref[...]


def _modulation(cond, w_mod, b_mod):
    depth, d, d3 = w_mod.shape
    bn = 1536
    return pl.pallas_call(
        _mod_body,
        grid=(depth, d3 // bn),
        in_specs=[
            pl.BlockSpec((MOD_ROWS, d), lambda l, j: (0, 0)),
            pl.BlockSpec((None, d, bn), lambda l, j: (l, 0, j)),
            pl.BlockSpec((None, 1, bn), lambda l, j: (l, 0, j)),
        ],
        out_specs=pl.BlockSpec((None, MOD_ROWS, bn), lambda l, j: (l, 0, j)),
        out_shape=jax.ShapeDtypeStruct((depth, MOD_ROWS, d3), F32),
        compiler_params=_params("arbitrary", "arbitrary"),
        name="modulation",
    )(cond, w_mod, b_mod.reshape(depth, 1, d3))


def _in_proj_body(x_ref, mod_ref, w_ref, o_ref):
    d = x_ref.shape[-1]
    shift = mod_ref[:, 0:d]
    scale = mod_ref[:, d:2 * d]
    h = (x_ref[...] * (1.0 + scale) + shift).astype(BF16)
    o_ref[...] = _dot(h, w_ref[...])


def _in_proj(x, mod, row0, w_in_bf, layer):
    g, t, d = x.shape
    d_in = w_in_bf.shape[-1]
    bm, bn = 1024, 1024
    return pl.pallas_call(
        _in_proj_body,
        grid=(g, t // bm, d_in // bn),
        in_specs=[
            pl.BlockSpec((None, bm, d), lambda b, i, j: (b, i, 0)),
            pl.BlockSpec((None, None, 1, 3 * d), lambda b, i, j: (layer, row0 + b, 0, 0)),
            pl.BlockSpec((None, d, bn), lambda b, i, j: (layer, 0, j)),
        ],
        out_specs=pl.BlockSpec((None, bm, bn), lambda b, i, j: (b, i, j)),
        out_shape=jax.ShapeDtypeStruct((g, t, d_in), F32),
        compiler_params=_params("arbitrary", "arbitrary", "arbitrary"),
        name="in_proj",
    )(x, mod, w_in_bf)


def _cs(k, n):
    ang = 2.0 * np.pi * (np.asarray(k, dtype=np.int64) % n).astype(np.float64) / n
    return np.cos(ang), np.sin(ang)


@functools.lru_cache(maxsize=None)
def _two_stage_tables(seq):
    n = 2 * seq
    n2 = DFT_INNER
    n1 = n // n2
    h1 = n1 // 2
    t1 = np.arange(h1)
    f1 = np.arange(n1)
    t2 = np.arange(n2)
    f2 = np.arange(n2)
    k = t1[None, None, :] * f1[None, :, None] * n2 + t2[:, None, None] * f1[None, :, None]
    c, s = _cs(k, n)
    fa = np.concatenate([c, -s], axis=1)
    c, s = _cs(f2[:, None] * t2[None, :], n2)
    fb = np.block([[c, s], [-s, c]])
    c, s = _cs(t2[:, None] * f2[None, :], n2)
    fbi = np.block([[c, -s], [s, c]])
    t1o = np.arange(h1) + h1 // 2
    k = t1o[None, :, None] * f1[None, None, :] * n2 + t2[:, None, None] * f1[None, None, :]
    c, s = _cs(k, n)
    ga = np.concatenate([c, -s], axis=2)
    return (jnp.asarray(fa, BF16), jnp.asarray(fb, BF16), jnp.asarray(fbi, BF16), jnp.asarray(ga, BF16))


@functools.lru_cache(maxsize=None)
def _one_stage_tables(seq):
    n = 2 * seq
    f = np.arange(n)
    t = np.arange(seq)
    c, s = _cs(f[:, None] * t[None, :], n)
    fwd = np.concatenate([c, -s], axis=0)
    to = np.arange(seq) + seq // 2
    c, s = _cs(to[:, None] * f[None, :], n)
    inv = np.concatenate([c, -s], axis=1)
    return jnp.asarray(fwd, BF16), jnp.asarray(inv, BF16)


@functools.lru_cache(maxsize=None)
def _filter_constants(seq):
    pos = np.arange(seq, dtype=np.float32)
    t = pos / np.float32(max(seq - 1, 1))
    bands = (FILT_EMB - 1) // 2
    fb = np.linspace(1e-4, bands - 1, bands, dtype=np.float32)
    ang = np.float32(2.0 * math.pi / seq) * pos[:, None] * fb[None, :]
    feats = np.zeros((seq, FILT_HID), np.float32)
    feats[:, 0] = t
    feats[:, 1:1 + bands] = np.cos(ang)
    feats[:, 1 + bands:FILT_EMB] = -np.sin(ang)
    dist = np.abs(pos - (seq // 2)) / np.float32(seq / 2.0)
    dist = np.broadcast_to(dist[:, None], (seq, LANES)).astype(np.float32)
    deltas = np.abs(np.linspace(math.log(HY_TARGET) / HY_SLOW_DECAY, math.log(HY_TARGET) / HY_FAST_DECAY,
                                HY_C, dtype=np.float32))
    deltas = np.tile(deltas, HY_ORDER)[None, :]
    return jnp.asarray(feats), jnp.asarray(dist), jnp.asarray(deltas)


def _implicit_filter(feats_ref, dist_ref, delt_ref, w1_ref, b1_ref, w2_ref, b2_ref, w3_ref, fr_ref):
    fr = fr_ref[...]
    h = jnp.sin(fr * (_dot_exact(feats_ref[...], w1_ref[...]) + b1_ref[...]))
    h = jnp.sin(fr * (_dot_exact(h, w2_ref[...]) + b2_ref[...]))
    h = _dot_exact(h, w3_ref[...])
    return h * jnp.exp(-dist_ref[...] * delt_ref[...])


def _forward_two_stage(src_ref, a_ref, fa_ref, n1, emit):
    n2 = DFT_INNER
    h1 = n1 // 2

    def stage_a(t2, carry):
        xs = src_ref[pl.ds(t2, h1, stride=n2), :].astype(BF16)
        a_ref[pl.ds(pl.multiple_of(t2 * 2 * n1, 2 * n1), 2 * n1), :] = _dot(fa_ref[t2], xs)
        return carry

    lax.fori_loop(0, n2, stage_a, 0)

    def stage_b(f1, carry):
        ar = a_ref[pl.ds(f1, n2, stride=2 * n1), :]
        ai = a_ref[pl.ds(n1 + f1, n2, stride=2 * n1), :]
        emit(f1, jnp.concatenate([ar, ai], axis=0).astype(BF16))
        return carry

    lax.fori_loop(0, n1, stage_b, 0)


def _filter_two_stage_body(feats_ref, dist_ref, delt_ref, w1_ref, b1_ref, w2_ref, b2_ref, w3_ref, fr_ref,
                           fa_ref, fb_ref, o_ref, h_ref, a_ref, *, n1):
    h_ref[...] = _implicit_filter(feats_ref, dist_ref, delt_ref, w1_ref, b1_ref, w2_ref, b2_ref, w3_ref, fr_ref)
    inv_n = 1.0 / (n1 * DFT_INNER)

    def emit(f1, z):
        o_ref[f1] = (_dot(fb_ref[...], z) * inv_n).astype(o_ref.dtype)

    _forward_two_stage(h_ref, a_ref, fa_ref, n1, emit)


def _filter_one_stage_body(feats_ref, dist_ref, delt_ref, w1_ref, b1_ref, w2_ref, b2_ref, w3_ref, fr_ref,
                           fwd_ref, o_ref):
    h = _implicit_filter(feats_ref, dist_ref, delt_ref, w1_ref, b1_ref, w2_ref, b2_ref, w3_ref, fr_ref)
    inv_n = 1.0 / fwd_ref.shape[0] * 2.0
    o_ref[...] = (_dot(fwd_ref[...], h.astype(BF16)) * inv_n).astype(o_ref.dtype)


def _filter_weight_specs(cw):
    return [
        pl.BlockSpec((None, FILT_HID, FILT_HID), lambda l, j: (l, 0, 0)),
        pl.BlockSpec((None, 1, FILT_HID), lambda l, j: (l, 0, 0)),
        pl.BlockSpec((None, FILT_HID, FILT_HID), lambda l, j: (l, 0, 0)),
        pl.BlockSpec((None, 1, FILT_HID), lambda l, j: (l, 0, 0)),
        pl.BlockSpec((None, FILT_HID, cw), lambda l, j: (l, 0, j)),
        pl.BlockSpec((None, 1, FILT_HID), lambda l, j: (l, 0, 0)),
    ]


def _filter_weights(hy_f_w1, hy_f_b1, hy_f_w2, hy_f_b2, hy_f_w3, hy_f_freq):
    depth = hy_f_w1.shape[0]
    w1 = jnp.pad(hy_f_w1, ((0, 0), (0, FILT_HID - FILT_EMB), (0, 0)))
    return (w1, hy_f_b1.reshape(depth, 1, FILT_HID), hy_f_w2, hy_f_b2.reshape(depth, 1, FILT_HID),
            hy_f_w3, hy_f_freq.reshape(depth, 1, FILT_HID))


def _filter_spectra_two_stage(seq, fweights):
    depth = fweights[0].shape[0]
    cw = LANES
    n2 = DFT_INNER
    n1 = 2 * seq // n2
    feats, dist, deltas = _filter_constants(seq)
    fa, fb, _, _ = _two_stage_tables(seq)
    ncol = HY_ORDER * HY_C
    return pl.pallas_call(
        functools.partial(_filter_two_stage_body, n1=n1),
        grid=(depth, ncol // cw),
        in_specs=[
            pl.BlockSpec((seq, FILT_HID), lambda l, j: (0, 0)),
            pl.BlockSpec((seq, LANES), lambda l, j: (0, 0)),
            pl.BlockSpec((1, cw), lambda l, j: (0, j)),
            *_filter_weight_specs(cw),
            pl.BlockSpec(fa.shape, lambda l, j: (0, 0, 0)),
            pl.BlockSpec(fb.shape, lambda l, j: (0, 0)),
        ],
        out_specs=pl.BlockSpec((None, n1, 2 * n2, cw), lambda l, j: (l, 0, 0, j)),
        out_shape=jax.ShapeDtypeStruct((depth, n1, 2 * n2, ncol), BF16),
        scratch_shapes=[pltpu.VMEM((seq, cw), F32), pltpu.VMEM((n2 * 2 * n1, cw), F32)],
        compiler_params=_params("arbitrary", "arbitrary"),
        name="hyena_filter_long",
    )(feats, dist, deltas, *fweights, fa, fb)


def _filter_spectra_one_stage(seq, fweights):
    depth = fweights[0].shape[0]
    cw = LANES
    feats, dist, deltas = _filter_constants(seq)
    fwd, _ = _one_stage_tables(seq)
    ncol = HY_ORDER * HY_C
    return pl.pallas_call(
        _filter_one_stage_body,
        grid=(depth, ncol // cw),
        in_specs=[
            pl.BlockSpec((seq, FILT_HID), lambda l, j: (0, 0)),
            pl.BlockSpec((seq, LANES), lambda l, j: (0, 0)),
            pl.BlockSpec((1, cw), lambda l, j: (0, j)),
            *_filter_weight_specs(cw),
            pl.BlockSpec(fwd.shape, lambda l, j: (0, 0)),
        ],
        out_specs=pl.BlockSpec((None, 4 * seq, cw), lambda l, j: (l, 0, j)),
        out_shape=jax.ShapeDtypeStruct((depth, 4 * seq, ncol), BF16),
        compiler_params=_params("arbitrary", "arbitrary"),
        name="hyena_filter_short",
    )(feats, dist, deltas, *fweights, fwd)


def _short_conv(x, w_ref, b_ref, part):
    seq = x.shape[0]
    rows = lax.broadcasted_iota(jnp.int32, x.shape, 0)
    prev = jnp.where(rows == 0, 0.0, pltpu.roll(x, 1, 0))
    nxt = jnp.where(rows == seq - 1, 0.0, pltpu.roll(x, seq - 1, 0))
    w = w_ref[part]
    return prev * w[0:1] + x * w[1:2] + nxt * w[2:3] + b_ref[part:part + 1, :]


def _complex_mul(x, h, half):
    xr, xi = x[:half], x[half:]
    hr, hi = h[:half].astype(F32), h[half:].astype(F32)
    return jnp.concatenate([xr * hr - xi * hi, xr * hi + xi * hr], axis=0)


def _hyena_long_body(v_ref, x1_ref, x2_ref, g_ref, cw_ref, cb_ref, skip_ref, hf0_ref, hf1_ref,
                     fa_ref, fb_ref, fbi_ref, ga_ref, o_ref, y_ref, c_ref, a_ref, w_ref, *, n1):
    n2 = DFT_INNER
    h1 = n1 // 2

    def long_conv(hf_ref):
        def emit(f1, z):
            spec = _complex_mul(_dot(fb_ref[...], z), hf_ref[f1], n2)
            w_ref[pl.ds(pl.multiple_of(f1 * 2 * n2, 2 * n2), 2 * n2), :] = _dot(fbi_ref[...], spec.astype(BF16))

        _forward_two_stage(y_ref, a_ref, fa_ref, n1, emit)

        def stage_c(t2, carry):
            wr = w_ref[pl.ds(t2, n1, stride=2 * n2), :]
            wi = w_ref[pl.ds(n2 + t2, n1, stride=2 * n2), :]
            wc = jnp.concatenate([wr, wi], axis=0).astype(BF16)
            c_ref[pl.ds(t2, h1, stride=n2), :] = _dot(ga_ref[t2], wc)
            return carry

        lax.fori_loop(0, n2, stage_c, 0)

    y_ref[...] = _short_conv(v_ref[...], cw_ref, cb_ref, 0)
    long_conv(hf0_ref)
    y = y_ref[...]
    y_ref[...] = _short_conv(x1_ref[...], cw_ref, cb_ref, 1) * (c_ref[...] + skip_ref[0:1, :] * y)
    long_conv(hf1_ref)
    y = y_ref[...]
    out = _short_conv(x2_ref[...], cw_ref, cb_ref, 2) * (c_ref[...] + skip_ref[1:2, :] * y)
    o_ref[...] = (out * _silu(g_ref[...])).astype(o_ref.dtype)


def _const_spec(shape):
    nd = len(shape)
    return pl.BlockSpec(shape, lambda *_: (0,) * nd, pipeline_mode=pl.Buffered(1))


def _hyena_long(p, conv_w, conv_b, skip, spectra, layer):
    b, seq, _ = p.shape
    cw = LANES
    n2 = DFT_INNER
    n1 = 2 * seq // n2
    nblk = HY_C // cw
    fa, fb, fbi, ga = _two_stage_tables(seq)

    def pcol(off):
        return pl.BlockSpec((None, seq, cw), lambda j, i: (i, 0, off // cw + j))

    def hf_spec(order):
        return pl.BlockSpec((None, n1, 2 * n2, cw), lambda j, i: (layer, 0, 0, order * nblk + j),
                            pipeline_mode=pl.Buffered(1))

    return pl.pallas_call(
        functools.partial(_hyena_long_body, n1=n1),
        grid=(nblk, b),
        in_specs=[
            pcol(OFF_HYZ), pcol(OFF_HYZ + HY_C), pcol(OFF_HYZ + 2 * HY_C), pcol(OFF_HYG),
            pl.BlockSpec((None, HY_ORDER + 1, SHORT_K, cw), lambda j, i: (layer, 0, 0, j)),
            pl.BlockSpec((None, HY_ORDER + 1, cw), lambda j, i: (layer, 0, j)),
            pl.BlockSpec((None, HY_ORDER, cw), lambda j, i: (layer, 0, j)),
            hf_spec(0), hf_spec(1),
            _const_spec(fa.shape), _const_spec(fb.shape), _const_spec(fbi.shape), _const_spec(ga.shape),
        ],
        out_specs=pl.BlockSpec((None, seq, cw), lambda j, i: (i, 0, j)),
        out_shape=jax.ShapeDtypeStruct((b, seq, HY_C), BF16),
        scratch_shapes=[
            pltpu.VMEM((seq, cw), F32), pltpu.VMEM((seq, cw), F32),
            pltpu.VMEM((n2 * 2 * n1, cw), F32), pltpu.VMEM((n1 * 2 * n2, cw), F32),
        ],
        compiler_params=_params("arbitrary", "arbitrary"),
        name="hyena_long",
    )(p, p, p, p, conv_w, conv_b, skip, spectra, spectra, fa, fb, fbi, ga)


def _hyena_short_body(v_ref, x1_ref, x2_ref, g_ref, cw_ref, cb_ref, skip_ref, hf0_ref, hf1_ref,
                      fwd_ref, inv_ref, o_ref):
    n = fwd_ref.shape[0] // 2

    def long_conv(y, hf_ref):
        spec = _complex_mul(_dot(fwd_ref[...], y.astype(BF16)), hf_ref[...], n)
        return _dot(inv_ref[...], spec.astype(BF16))

    y = _short_conv(v_ref[...], cw_ref, cb_ref, 0)
    y = _short_conv(x1_ref[...], cw_ref, cb_ref, 1) * (long_conv(y, hf0_ref) + skip_ref[0:1, :] * y)
    y = _short_conv(x2_ref[...], cw_ref, cb_ref, 2) * (long_conv(y, hf1_ref) + skip_ref[1:2, :] * y)
    o_ref[...] = (y * _silu(g_ref[...])).astype(o_ref.dtype)


def _hyena_short(p, conv_w, conv_b, skip, spectra, layer):
    b, seq, _ = p.shape
    cw = LANES
    nblk = HY_C // cw
    fwd, inv = _one_stage_tables(seq)

    def pcol(off):
        return pl.BlockSpec((None, seq, cw), lambda j, i: (i, 0, off // cw + j))

    def hf_spec(order):
        return pl.BlockSpec((None, 4 * seq, cw), lambda j, i: (layer, 0, order * nblk + j))

    return pl.pallas_call(
        _hyena_short_body,
        grid=(nblk, b),
        in_specs=[
            pcol(OFF_HYZ), pcol(OFF_HYZ + HY_C), pcol(OFF_HYZ + 2 * HY_C), pcol(OFF_HYG),
            pl.BlockSpec((None, HY_ORDER + 1, SHORT_K, cw), lambda j, i: (layer, 0, 0, j)),
            pl.BlockSpec((None, HY_ORDER + 1, cw), lambda j, i: (layer, 0, j)),
            pl.BlockSpec((None, HY_ORDER, cw), lambda j, i: (layer, 0, j)),
            hf_spec(0), hf_spec(1),
            pl.BlockSpec(fwd.shape, lambda j, i: (0, 0)), pl.BlockSpec(inv.shape, lambda j, i: (0, 0)),
        ],
        out_specs=pl.BlockSpec((None, seq, cw), lambda j, i: (i, 0, j)),
        out_shape=jax.ShapeDtypeStruct((b, seq, HY_C), BF16),
        compiler_params=_params("arbitrary", "arbitrary"),
        name="hyena_short",
    )(p, p, p, p, conv_w, conv_b, skip, spectra, spectra, fwd, inv)


def _chunk_mlp_body(u_ref, v_ref, g_ref, lg_ref, lb_ref, ws_ref, bs_ref, o_ref):
    v = v_ref[...]
    mu = jnp.mean(v, axis=-1, keepdims=True)
    vc = v - mu
    var = jnp.mean(vc * vc, axis=-1, keepdims=True)
    vn = (vc * lax.rsqrt(var + EPS) * lg_ref[...] + lb_ref[...]).astype(BF16)
    rows = v.shape[0]
    for n in range(rows // CHUNK):
        r = slice(n * CHUNK, (n + 1) * CHUNK)
        for h in range(CM_HEADS):
            c = slice(h * CM_HD, (h + 1) * CM_HD)
            s = _dot(ws_ref[h].astype(BF16), vn[r, c]) + bs_ref[:, c]
            o_ref[r, c] = (u_ref[r, c] * s * _silu(g_ref[r, c])).astype(o_ref.dtype)


def _chunk_mlp(p, ln_g, ln_b, w_s, b_s, layer):
    b, seq, _ = p.shape
    bm = min(seq, 512)
    bs_full = jnp.repeat(jnp.swapaxes(b_s, 1, 2), CM_HD, axis=2)

    def pcol(off):
        return pl.BlockSpec((None, bm, CM_C), lambda i, t: (i, t, off // CM_C))

    return pl.pallas_call(
        _chunk_mlp_body,
        grid=(b, seq // bm),
        in_specs=[
            pcol(OFF_CMU), pcol(OFF_CMV), pcol(OFF_CMG),
            pl.BlockSpec((None, 1, CM_C), lambda i, t: (layer, 0, 0)),
            pl.BlockSpec((None, 1, CM_C), lambda i, t: (layer, 0, 0)),
            pl.BlockSpec((None, CM_HEADS, CHUNK, CHUNK), lambda i, t: (layer, 0, 0, 0)),
            pl.BlockSpec((None, CHUNK, CM_C), lambda i, t: (layer, 0, 0)),
        ],
        out_specs=pl.BlockSpec((None, bm, CM_C), lambda i, t: (i, t, 0)),
        out_shape=jax.ShapeDtypeStruct((b, seq, CM_C), BF16),
        compiler_params=_params("arbitrary", "arbitrary"),
        name="chunk_mlp",
    )(p, p, p, ln_g.reshape(-1, 1, CM_C), ln_b.reshape(-1, 1, CM_C), w_s, bs_full)


def _rms_heads(x, g, nheads):
    outs = []
    for h in range(nheads):
        xh = x[:, h * HEAD_DIM:(h + 1) * HEAD_DIM]
        ms = jnp.mean(xh * xh, axis=-1, keepdims=True)
        outs.append(xh * lax.rsqrt(ms + EPS) * g)
    return outs


def _rope(xh, cos, sin):
    lane = lax.broadcasted_iota(jnp.int32, xh.shape, 1)
    quarter = ROPE_AXIS // 2
    partner = jnp.where((lane % ROPE_AXIS) < quarter,
                        pltpu.roll(xh, HEAD_DIM - quarter, 1), pltpu.roll(xh, quarter, 1))
    return xh * cos + partner * sin


def _kv_prep_body(k_ref, v_ref, gk_ref, cos_ref, sin_ref, kn_ref, kr_ref, vb_ref, *, rotary):
    heads = _rms_heads(k_ref[...], gk_ref[...], N_KV_HEADS)
    for h, kh in enumerate(heads):
        c = slice(h * HEAD_DIM, (h + 1) * HEAD_DIM)
        kn_ref[:, c] = kh
        if rotary:
            kh = _rope(kh, cos_ref[...], sin_ref[...])
        kr_ref[:, c] = kh.astype(BF16)
    vb_ref[...] = v_ref[...].astype(BF16)


def _kv_prep(p, k_norm_g, cos, sin, layer, rotary):
    b, seq, _ = p.shape
    bm = min(seq, 512)
    blk = pl.BlockSpec((None, bm, KV_C), lambda i, t: (i, t, 0))
    return pl.pallas_call(
        functools.partial(_kv_prep_body, rotary=rotary),
        grid=(b, seq // bm),
        in_specs=[
            pl.BlockSpec((None, bm, KV_C), lambda i, t: (i, t, OFF_K // KV_C)),
            pl.BlockSpec((None, bm, KV_C), lambda i, t: (i, t, OFF_V // KV_C)),
            pl.BlockSpec((None, 1, HEAD_DIM), lambda i, t: (layer, 0, 0)),
            pl.BlockSpec((bm, HEAD_DIM), lambda i, t: (t, 0)),
            pl.BlockSpec((bm, HEAD_DIM), lambda i, t: (t, 0)),
        ],
        out_specs=[blk, blk, blk],
        out_shape=[jax.ShapeDtypeStruct((b, seq, KV_C), F32),
                   jax.ShapeDtypeStruct((b, seq, KV_C), BF16),
                   jax.ShapeDtypeStruct((b, seq, KV_C), BF16)],
        compiler_params=_params("arbitrary", "arbitrary"),
        name="kv_prep",
    )(p, p, k_norm_g.reshape(-1, 1, HEAD_DIM), cos, sin)


def _attention_body(*refs, rotary, has_ctx):
    q_ref, g_ref, gq_ref, cos_ref, sin_ref, k_ref, v_ref = refs[:7]
    if has_ctx:
        ck_ref, cv_ref, o_ref = refs[7:]
    else:
        (o_ref,) = refs[7:]
    tq = q_ref.shape[0]
    scale = 1.0 / math.sqrt(HEAD_DIM)
    heads = _rms_heads(q_ref[...], gq_ref[...], GQA)
    if rotary:
        heads = [_rope(qh, cos_ref[...], sin_ref[...]) for qh in heads]
    q = jnp.concatenate([(qh * scale).astype(BF16) for qh in heads], axis=0)
    contract_last = (((1,), (1,)), ((), ()))
    s = lax.dot_general(q, k_ref[...], contract_last, preferred_element_type=F32)
    m = jnp.max(s, axis=-1, keepdims=True)
    if has_ctx:
        sc = lax.dot_general(q, ck_ref[...].astype(BF16), contract_last, preferred_element_type=F32)
        m = jnp.maximum(m, jnp.max(sc, axis=-1, keepdims=True))
    e = jnp.exp(s - m)
    den = jnp.sum(e, axis=-1, keepdims=True)
    o = _dot(e.astype(BF16), v_ref[...])
    if has_ctx:
        ec = jnp.exp(sc - m)
        den = den + jnp.sum(ec, axis=-1, keepdims=True)
        o = o + _dot(ec.astype(BF16), cv_ref[...].astype(BF16))
    o = o / den
    for h in range(GQA):
        c = slice(h * HEAD_DIM, (h + 1) * HEAD_DIM)
        o_ref[:, c] = (o[h * tq:(h + 1) * tq] * _silu(g_ref[:, c])).astype(o_ref.dtype)


def _attention(p, q_norm_g, cos, sin, keys, values, ctx_k, ctx_v, layer, rotary, tq):
    b, seq, _ = p.shape
    skv = keys.shape[1]
    gw = GQA * HEAD_DIM
    has_ctx = ctx_k is not None
    in_specs = [
        pl.BlockSpec((None, tq, gw), lambda i, h, t: (i, t, OFF_Q // gw + h)),
        pl.BlockSpec((None, tq, gw), lambda i, h, t: (i, t, OFF_ATG // gw + h)),
        pl.BlockSpec((None, 1, HEAD_DIM), lambda i, h, t: (layer, 0, 0)),
        pl.BlockSpec((tq, HEAD_DIM), lambda i, h, t: (t, 0)),
        pl.BlockSpec((tq, HEAD_DIM), lambda i, h, t: (t, 0)),
        pl.BlockSpec((None, skv, HEAD_DIM), lambda i, h, t: (i, 0, h)),
        pl.BlockSpec((None, skv, HEAD_DIM), lambda i, h, t: (i, 0, h)),
    ]
    args = [p, p, q_norm_g.reshape(-1, 1, HEAD_DIM), cos, sin, keys, values]
    if has_ctx:
        past = ctx_k.shape[2]
        spec = pl.BlockSpec((None, None, past, HEAD_DIM), lambda i, h, t: (i, layer, 0, h))
        in_specs += [spec, spec]
        args += [ctx_k, ctx_v]
    return pl.pallas_call(
        functools.partial(_attention_body, rotary=rotary, has_ctx=has_ctx),
        grid=(b, N_KV_HEADS, seq // tq),
        in_specs=in_specs,
        out_specs=pl.BlockSpec((None, tq, gw), lambda i, h, t: (i, t, h)),
        out_shape=jax.ShapeDtypeStruct((b, seq, ATT_C), BF16),
        compiler_params=_params("arbitrary", "arbitrary", "arbitrary"),
        name="attention",
    )(*args)


def _out_proj_body(yh_ref, yc_ref, ya_ref, x_ref, mod_ref, w_ref, lg_ref, lb_ref, o_ref):
    d = x_ref.shape[-1]
    out = _dot(yh_ref[...], w_ref[0:HY_C, :])
    out = out + _dot(yc_ref[...], w_ref[HY_C:HY_C + CM_C, :])
    out = out + _dot(ya_ref[...], w_ref[HY_C + CM_C:D_MIX, :])
    gate = mod_ref[:, 2 * d:3 * d]
    r = DN_ALPHA * x_ref[...] + gate * out
    mu = jnp.mean(r, axis=-1, keepdims=True)
    rc = r - mu
    var = jnp.mean(rc * rc, axis=-1, keepdims=True)
    o_ref[...] = rc * lax.rsqrt(var + EPS) * lg_ref[...] + lb_ref[...]


def _out_proj(y_hy, y_cm, y_at, x, mod, row0, w_out_bf, ln_g, ln_b, layer):
    g, t, d = x.shape
    bm = 512

    def rows(c):
        return pl.BlockSpec((None, bm, c), lambda b, i: (b, i, 0))

    return pl.pallas_call(
        _out_proj_body,
        grid=(g, t // bm),
        in_specs=[
            rows(HY_C), rows(CM_C), rows(ATT_C), rows(d),
            pl.BlockSpec((None, None, 1, 3 * d), lambda b, i: (layer, row0 + b, 0, 0)),
            pl.BlockSpec((None, D_MIX, d), lambda b, i: (layer, 0, 0)),
            pl.BlockSpec((None, 1, d), lambda b, i: (layer, 0, 0)),
            pl.BlockSpec((None, 1, d), lambda b, i: (layer, 0, 0)),
        ],
        out_specs=rows(d),
        out_shape=jax.ShapeDtypeStruct((g, t, d), F32),
        compiler_params=_params("arbitrary", "arbitrary"),
        name="out_proj_ln",
    )(y_hy, y_cm, y_at, x, mod, w_out_bf, ln_g.reshape(-1, 1, d), ln_b.reshape(-1, 1, d))


@functools.lru_cache(maxsize=None)
def _rope_tables(seq):
    rows = seq // GRID_W
    row = np.repeat(np.arange(rows, dtype=np.float32), GRID_W)
    col = np.tile(np.arange(GRID_W, dtype=np.float32), rows)
    inv = (np.float32(ROPE_THETA) ** (-np.arange(0, ROPE_AXIS, 2, dtype=np.float32) / np.float32(ROPE_AXIS))
           ).astype(np.float32)
    ra = row[:, None] * inv[None, :]
    ca = col[:, None] * inv[None, :]
    cos = np.concatenate([np.cos(ra), np.cos(ra), np.cos(ca), np.cos(ca)], axis=1)
    sin = np.concatenate([-np.sin(ra), np.sin(ra), -np.sin(ca), np.sin(ca)], axis=1)
    return jnp.asarray(cos, F32), jnp.asarray(sin, F32)


def _mixer_layer(x, seq, mod, row0, layer, rotary, w_in_bf, w_out_bf, spectra, hy_conv_w, hy_conv_b, hy_skip,
                 cm_ln_g, cm_ln_b, cm_w_s, cm_b_s, q_norm_g, k_norm_g, ln_g, ln_b, ctx_k, ctx_v, cos, sin):
    g, t, d = x.shape
    nreq = g * t // seq
    p = _in_proj(x, mod, row0, w_in_bf, layer).reshape(nreq, seq, D_IN)
    if rotary:
        y_hy = _hyena_long(p, hy_conv_w, hy_conv_b, hy_skip, spectra, layer)
    else:
        y_hy = _hyena_short(p, hy_conv_w, hy_conv_b, hy_skip, spectra, layer)
    y_cm = _chunk_mlp(p, cm_ln_g, cm_ln_b, cm_w_s, cm_b_s, layer)
    k_normed, keys, values = _kv_prep(p, k_norm_g, cos, sin, layer, rotary)
    y_at = _attention(p, q_norm_g, cos, sin, keys, values, ctx_k, ctx_v, layer, rotary,
                      tq=min(seq, 128))
    x_new = _out_proj(y_hy.reshape(g, t, HY_C), y_cm.reshape(g, t, CM_C), y_at.reshape(g, t, ATT_C),
                      x, mod, row0, w_out_bf, ln_g, ln_b, layer)
    return x_new, k_normed, p


def kernel(x_prompt, x_sample, cache_k, cache_v, c, c_ctx, w_mod, b_mod, w_in, hy_conv_w, hy_conv_b, hy_f_w1, hy_f_b1, hy_f_w2, hy_f_b2, hy_f_w3, hy_f_freq, hy_skip, cm_ln_g, cm_ln_b, cm_w_s, cm_b_s, q_norm_g, k_norm_g, w_out, ln_g, ln_b):
    batch, seq, d = x_prompt.shape
    dec_batch, dec_seq, _ = x_sample.shape
    depth = w_in.shape[0]
    past = cache_k.shape[2]

    cond = jnp.concatenate([c_ctx[None, :], c, jnp.zeros((MOD_ROWS - 1 - dec_batch, d), F32)], axis=0)
    mod = _modulation(cond, w_mod, b_mod).reshape(depth, MOD_ROWS, 1, 3 * d)

    w_in_bf = w_in.astype(BF16)
    w_out_bf = w_out.astype(BF16)
    fweights = _filter_weights(hy_f_w1, hy_f_b1, hy_f_w2, hy_f_b2, hy_f_w3, hy_f_freq)
    spectra_ctx = _filter_spectra_one_stage(seq, fweights)
    spectra_lat = _filter_spectra_two_stage(dec_seq, fweights)
    conv_w = hy_conv_w.reshape(depth, SHORT_K, HY_ORDER + 1, HY_C).transpose(0, 2, 1, 3)
    conv_b = hy_conv_b.reshape(depth, HY_ORDER + 1, HY_C)
    cos, sin = _rope_tables(dec_seq)
    ctx_k = cache_k.reshape(dec_batch, depth, past, KV_C)
    ctx_v = cache_v.reshape(dec_batch, depth, past, KV_C)

    xp = x_prompt.reshape(1, batch * seq, d)
    xs = x_sample
    new_k, new_v = [], []
    for layer in range(depth):
        common = dict(layer=layer, w_in_bf=w_in_bf, w_out_bf=w_out_bf, hy_conv_w=conv_w, hy_conv_b=conv_b,
                      hy_skip=hy_skip, cm_ln_g=cm_ln_g, cm_ln_b=cm_ln_b, cm_w_s=cm_w_s, cm_b_s=cm_b_s,
                      q_norm_g=q_norm_g, k_norm_g=k_norm_g, ln_g=ln_g, ln_b=ln_b, cos=cos, sin=sin)
        xp, k_l, p_ctx = _mixer_layer(xp, seq, mod, 0, rotary=False, spectra=spectra_ctx,
                                      ctx_k=None, ctx_v=None, **common)
        new_k.append(k_l.reshape(batch, seq, N_KV_HEADS, HEAD_DIM))
        new_v.append(p_ctx[:, :, OFF_V:OFF_V + KV_C].reshape(batch, seq, N_KV_HEADS, HEAD_DIM))
        xs, _, _ = _mixer_layer(xs, dec_seq, mod, 1, rotary=True, spectra=spectra_lat,
                                ctx_k=ctx_k, ctx_v=ctx_v, **common)
    return (xp.reshape(batch, seq, d), xs, jnp.stack(new_k, axis=1), jnp.stack(new_v, axis=1))
```

```python
import functools
import math

import numpy as np
import jax
import jax.numpy as jnp
from jax import lax
from jax.experimental import pallas as pl
from jax.experimental.pallas import tpu as pltpu

F32 = jnp.float32
BF16 = jnp.bfloat16

D_MODEL = 2048
DEPTH = 4
GRID_W = 64
HY_C = 512
HY_ORDER = 2
SHORT_K = 3
FILT_EMB = 33
FILT_HID = 64
HY_FAST_DECAY = 0.3
HY_SLOW_DECAY = 1.5
HY_TARGET = 1e-2
CM_HEADS = 4
CM_HD = 128
CM_C = CM_HEADS * CM_HD
CHUNK = 128
N_HEADS = 8
N_KV_HEADS = 2
HEAD_DIM = 128
GQA = N_HEADS // N_KV_HEADS
ATT_C = N_HEADS * HEAD_DIM
KV_C = N_KV_HEADS * HEAD_DIM
ROPE_THETA = 10000.0
ROPE_AXIS = HEAD_DIM // 2
D_MIX = HY_C + CM_C + ATT_C
EPS = 1e-6
DN_ALPHA = (2 * DEPTH) ** 0.25

OFF_HYZ = 0
OFF_HYG = (HY_ORDER + 1) * HY_C
OFF_CMU = OFF_HYG + HY_C
OFF_CMV = OFF_CMU + CM_C
OFF_CMG = OFF_CMV + CM_C
OFF_Q = OFF_CMG + CM_C
OFF_K = OFF_Q + ATT_C
OFF_V = OFF_K + KV_C
OFF_ATG = OFF_V + KV_C
D_IN = OFF_ATG + ATT_C

LANES = 128
VMEM_LIMIT_BYTES = 56 * 1024 * 1024

DFT_INNER = 128
ROW_GROUP = 8
LANE_PAIR = 2
ATT_KEY_CHUNK = 512
MOD_ROWS = 16


def _params(*semantics):
    return pltpu.CompilerParams(dimension_semantics=semantics, vmem_limit_bytes=VMEM_LIMIT_BYTES)


def _silu(x):
    return x * jax.nn.sigmoid(x)


def _dot(a, b):
    return jnp.dot(a, b, preferred_element_type=F32)


def _dot_exact(a, b):
    return jnp.dot(a, b, preferred_element_type=F32, precision=lax.Precision.HIGHEST)


def _mod_body(c_ref, w_ref, b_ref, o_ref):
    s = _silu(c_ref[...]).astype(BF16)
    o_ref[...] = _dot(s, w_ref[...].astype(BF16)) + b_ref[...]


def _modulation(cond, w_mod, b_mod):
    depth, d, d3 = w_mod.shape
    bn = 1536
    return pl.pallas_call(
        _mod_body,
        grid=(depth, d3 // bn),
        in_specs=[
            pl.BlockSpec((MOD_ROWS, d), lambda l, j: (0, 0)),
            pl.BlockSpec((None, d, bn), lambda l, j: (l, 0, j)),
            pl.BlockSpec((None, 1, bn), lambda l, j: (l, 0, j)),
        ],
        out_specs=pl.BlockSpec((None, MOD_ROWS, bn), lambda l, j: (l, 0, j)),
        out_shape=jax.ShapeDtypeStruct((depth, MOD_ROWS, d3), F32),
        compiler_params=_params("arbitrary", "arbitrary"),
        name="modulation",
    )(cond, w_mod, b_mod.reshape(depth, 1, d3))


def _in_proj_body(x_ref, mod_ref, w_ref, o_ref):
    d = x_ref.shape[-1]
    shift = mod_ref[:, 0:d]
    scale = mod_ref[:, d:2 * d]
    h = (x_ref[...] * (1.0 + scale) + shift).astype(BF16)
    o_ref[...] = _dot(h, w_ref[...])


def _in_proj(x, mod, row0, w_in_bf, layer):
    g, t, d = x.shape
    d_in = w_in_bf.shape[-1]
    bm, bn = 1024, 1024
    return pl.pallas_call(
        _in_proj_body,
        grid=(g, t // bm, d_in // bn),
        in_specs=[
            pl.BlockSpec((None, bm, d), lambda b, i, j: (b, i, 0)),
            pl.BlockSpec((None, None, 1, 3 * d), lambda b, i, j: (layer, row0 + b, 0, 0)),
            pl.BlockSpec((None, d, bn), lambda b, i, j: (layer, 0, j)),
        ],
        out_specs=pl.BlockSpec((None, bm, bn), lambda b, i, j: (b, i, j)),
        out_shape=jax.ShapeDtypeStruct((g, t, d_in), F32),
        compiler_params=_params("arbitrary", "arbitrary", "arbitrary"),
        name="in_proj",
    )(x, mod, w_in_bf)


def _cs(k, n):
    ang = 2.0 * np.pi * (np.asarray(k, dtype=np.int64) % n).astype(np.float64) / n
    return np.cos(ang), np.sin(ang)


@functools.lru_cache(maxsize=None)
def _two_stage_tables(seq):
    n = 2 * seq
    n2 = DFT_INNER
    n1 = n // n2
    h1 = n1 // 2
    g = ROW_GROUP
    t1 = np.arange(h1)
    f1 = np.arange(n1)
    t2 = np.arange(n2)
    f2 = np.arange(n2)
    lo = np.arange(g)
    eye = np.eye(g)
    k = t1[None, None, :, None] * f1[:, None, None, None] * n2 + lo[None, None, None, :] * f1[:, None, None, None]
    c, s = _cs(k, n)
    c = c * eye[None, :, None, :]
    s = s * eye[None, :, None, :]
    ma = np.concatenate([c.reshape(n1 * g, h1 * g), -s.reshape(n1 * g, h1 * g)], axis=0)
    hi = np.arange(n2 // g)
    c, s = _cs(g * hi[:, None] * f1[None, :], n)
    twc = np.broadcast_to(c[:, :, None], c.shape + (LANES,))
    tws = np.broadcast_to(-s[:, :, None], s.shape + (LANES,))
    c, s = _cs(f2[:, None] * t2[None, :], n2)
    fb = np.block([[c, s], [-s, c]])
    c, s = _cs(t2[:, None] * f2[None, :], n2)
    fbi = np.block([[c, -s], [s, c]])
    t1o = np.arange(h1) + h1 // 2
    k = t1o[:, None, None, None] * f1[None, None, :, None] * n2 + lo[None, None, None, :] * f1[None, None, :, None]
    c, s = _cs(k, n)
    c = c * eye[None, :, None, :]
    s = s * eye[None, :, None, :]
    mc = np.concatenate([c.reshape(h1 * g, n1 * g), -s.reshape(h1 * g, n1 * g)], axis=1)
    return (jnp.asarray(ma, BF16), jnp.asarray(fb, BF16), jnp.asarray(fbi, BF16), jnp.asarray(mc, BF16),
            jnp.asarray(twc, F32), jnp.asarray(tws, F32))


@functools.lru_cache(maxsize=None)
def _one_stage_tables(seq):
    n = 2 * seq
    f = np.arange(n)
    t = np.arange(seq)
    c, s = _cs(f[:, None] * t[None, :], n)
    fwd = np.concatenate([c, -s], axis=0)
    to = np.arange(seq) + seq // 2
    c, s = _cs(to[:, None] * f[None, :], n)
    inv = np.concatenate([c, -s], axis=1)
    return jnp.asarray(fwd, BF16), jnp.asarray(inv, BF16)


@functools.lru_cache(maxsize=None)
def _filter_constants(seq):
    pos = np.arange(seq, dtype=np.float32)
    t = pos / np.float32(max(seq - 1, 1))
    bands = (FILT_EMB - 1) // 2
    fb = np.linspace(1e-4, bands - 1, bands, dtype=np.float32)
    ang = np.float32(2.0 * math.pi / seq) * pos[:, None] * fb[None, :]
    feats = np.zeros((seq, FILT_HID), np.float32)
    feats[:, 0] = t
    feats[:, 1:1 + bands] = np.cos(ang)
    feats[:, 1 + bands:FILT_EMB] = -np.sin(ang)
    dist = np.abs(pos - (seq // 2)) / np.float32(seq / 2.0)
    dist = np.broadcast_to(dist[:, None], (seq, LANES)).astype(np.float32)
    deltas = np.abs(np.linspace(math.log(HY_TARGET) / HY_SLOW_DECAY, math.log(HY_TARGET) / HY_FAST_DECAY,
                                HY_C, dtype=np.float32))
    deltas = np.tile(deltas, HY_ORDER)[None, :]
    return jnp.asarray(feats), jnp.asarray(dist), jnp.asarray(deltas)


def _implicit_filter(feats_ref, dist_ref, delt_ref, w1_ref, b1_ref, w2_ref, b2_ref, w3_ref, fr_ref):
    fr = fr_ref[...]
    h = jnp.sin(fr * (_dot_exact(feats_ref[...], w1_ref[...]) + b1_ref[...]))
    h = jnp.sin(fr * (_dot_exact(h, w2_ref[...]) + b2_ref[...]))
    h = _dot_exact(h, w3_ref[...])
    return h * jnp.exp(-dist_ref[...] * delt_ref[...])


def _forward_two_stage(src_ref, a_ref, ma_ref, twc_ref, tws_ref, n1, emit):
    n2 = DFT_INNER
    g = ROW_GROUP
    h1 = n1 // 2
    cw = src_ref.shape[-1]
    half = n1 * g

    def stage_a(step, carry):
        his = [step * LANE_PAIR + j for j in range(LANE_PAIR)]
        xs = [src_ref[:, pl.ds(pl.multiple_of(hi * g, g), g), :].reshape(h1 * g, cw).astype(BF16) for hi in his]
        out = _dot(ma_ref[...], jnp.concatenate(xs, axis=1))
        for j, hi in enumerate(his):
            lanes = slice(j * cw, (j + 1) * cw)
            twc = twc_ref[hi]
            tws = tws_ref[hi]
            for f1 in range(n1):
                c = twc[f1:f1 + 1]
                s = tws[f1:f1 + 1]
                re = out[f1 * g:(f1 + 1) * g, lanes]
                im = out[half + f1 * g:half + (f1 + 1) * g, lanes]
                a_ref[hi, f1 * g:(f1 + 1) * g, :] = re * c - im * s
                a_ref[hi, half + f1 * g:half + (f1 + 1) * g, :] = re * s + im * c
        return carry

    lax.fori_loop(0, n2 // g // LANE_PAIR, stage_a, 0, unroll=2)

    def stage_b(step, carry):
        zs = []
        for j in range(LANE_PAIR):
            f1 = step * LANE_PAIR + j
            ar = a_ref[:, pl.ds(pl.multiple_of(f1 * g, g), g), :].reshape(n2, cw)
            ai = a_ref[:, pl.ds(pl.multiple_of(half + f1 * g, g), g), :].reshape(n2, cw)
            zs.append(jnp.concatenate([ar, ai], axis=0).astype(BF16))
        emit(step * LANE_PAIR, jnp.concatenate(zs, axis=1))
        return carry

    lax.fori_loop(0, n1 // LANE_PAIR, stage_b, 0, unroll=8)


def _filter_two_stage_body(feats_ref, dist_ref, delt_ref, w1_ref, b1_ref, w2_ref, b2_ref, w3_ref, fr_ref,
                           ma_ref, fb_ref, twc_ref, tws_ref, o_ref, h_ref, a_ref, *, n1):
    h = _implicit_filter(feats_ref, dist_ref, delt_ref, w1_ref, b1_ref, w2_ref, b2_ref, w3_ref, fr_ref)
    h_ref[...] = h.reshape(h_ref.shape)
    inv_n = 1.0 / (n1 * DFT_INNER)

    cw = h_ref.shape[-1]

    def emit(f1, z):
        spec = (_dot(fb_ref[...], z) * inv_n).astype(o_ref.dtype)
        for j in range(LANE_PAIR):
            o_ref[f1 + j] = spec[:, j * cw:(j + 1) * cw]

    _forward_two_stage(h_ref, a_ref, ma_ref, twc_ref, tws_ref, n1, emit)


def _filter_one_stage_body(feats_ref, dist_ref, delt_ref, w1_ref, b1_ref, w2_ref, b2_ref, w3_ref, fr_ref,
                           fwd_ref, o_ref):
    h = _implicit_filter(feats_ref, dist_ref, delt_ref, w1_ref, b1_ref, w2_ref, b2_ref, w3_ref, fr_ref)
    inv_n = 1.0 / fwd_ref.shape[0] * 2.0
    o_ref[...] = (_dot(fwd_ref[...], h.astype(BF16)) * inv_n).astype(o_ref.dtype)


def _filter_weight_specs(cw):
    return [
        pl.BlockSpec((None, FILT_HID, FILT_HID), lambda l, j: (l, 0, 0)),
        pl.BlockSpec((None, 1, FILT_HID), lambda l, j: (l, 0, 0)),
        pl.BlockSpec((None, FILT_HID, FILT_HID), lambda l, j: (l, 0, 0)),
        pl.BlockSpec((None, 1, FILT_HID), lambda l, j: (l, 0, 0)),
        pl.BlockSpec((None, FILT_HID, cw), lambda l, j: (l, 0, j)),
        pl.BlockSpec((None, 1, FILT_HID), lambda l, j: (l, 0, 0)),
    ]


def _filter_weights(hy_f_w1, hy_f_b1, hy_f_w2, hy_f_b2, hy_f_w3, hy_f_freq):
    depth = hy_f_w1.shape[0]
    w1 = jnp.pad(hy_f_w1, ((0, 0), (0, FILT_HID - FILT_EMB), (0, 0)))
    return (w1, hy_f_b1.reshape(depth, 1, FILT_HID), hy_f_w2, hy_f_b2.reshape(depth, 1, FILT_HID),
            hy_f_w3, hy_f_freq.reshape(depth, 1, FILT_HID))


def _filter_spectra_two_stage(seq, fweights):
    depth = fweights[0].shape[0]
    cw = LANES
    n2 = DFT_INNER
    n1 = 2 * seq // n2
    feats, dist, deltas = _filter_constants(seq)
    ma, fb, _, _, twc, tws = _two_stage_tables(seq)
    ncol = HY_ORDER * HY_C
    return pl.pallas_call(
        functools.partial(_filter_two_stage_body, n1=n1),
        grid=(depth, ncol // cw),
        in_specs=[
            pl.BlockSpec((seq, FILT_HID), lambda l, j: (0, 0)),
            pl.BlockSpec((seq, LANES), lambda l, j: (0, 0)),
            pl.BlockSpec((1, cw), lambda l, j: (0, j)),
            *_filter_weight_specs(cw),
            pl.BlockSpec(ma.shape, lambda l, j: (0, 0)),
            pl.BlockSpec(fb.shape, lambda l, j: (0, 0)),
            pl.BlockSpec(twc.shape, lambda l, j: (0, 0, 0)),
            pl.BlockSpec(tws.shape, lambda l, j: (0, 0, 0)),
        ],
        out_specs=pl.BlockSpec((None, n1, 2 * n2, cw), lambda l, j: (l, 0, 0, j)),
        out_shape=jax.ShapeDtypeStruct((depth, n1, 2 * n2, ncol), BF16),
        scratch_shapes=[pltpu.VMEM((n1 // 2, n2, cw), F32),
                        pltpu.VMEM((n2 // ROW_GROUP, 2 * n1 * ROW_GROUP, cw), F32)],
        compiler_params=_params("arbitrary", "arbitrary"),
        name="hyena_filter_long",
    )(feats, dist, deltas, *fweights, ma, fb, twc, tws)


def _filter_spectra_one_stage(seq, fweights):
    depth = fweights[0].shape[0]
    cw = LANES
    feats, dist, deltas = _filter_constants(seq)
    fwd, _ = _one_stage_tables(seq)
    ncol = HY_ORDER * HY_C
    return pl.pallas_call(
        _filter_one_stage_body,
        grid=(depth, ncol // cw),
        in_specs=[
            pl.BlockSpec((seq, FILT_HID), lambda l, j: (0, 0)),
            pl.BlockSpec((seq, LANES), lambda l, j: (0, 0)),
            pl.BlockSpec((1, cw), lambda l, j: (0, j)),
            *_filter_weight_specs(cw),
            pl.BlockSpec(fwd.shape, lambda l, j: (0, 0)),
        ],
        out_specs=pl.BlockSpec((None, 4 * seq, cw), lambda l, j: (l, 0, j)),
        out_shape=jax.ShapeDtypeStruct((depth, 4 * seq, ncol), BF16),
        compiler_params=_params("arbitrary", "arbitrary"),
        name="hyena_filter_short",
    )(feats, dist, deltas, *fweights, fwd)


def _short_conv(x, w_ref, b_ref, part):
    seq = x.shape[0]
    rows = lax.broadcasted_iota(jnp.int32, x.shape, 0)
    prev = jnp.where(rows == 0, 0.0, pltpu.roll(x, 1, 0))
    nxt = jnp.where(rows == seq - 1, 0.0, pltpu.roll(x, seq - 1, 0))
    w = w_ref[part]
    return prev * w[0:1] + x * w[1:2] + nxt * w[2:3] + b_ref[part:part + 1, :]


def _complex_mul(x, h, half):
    xr, xi = x[:half], x[half:]
    hr, hi = h[:half].astype(F32), h[half:].astype(F32)
    return jnp.concatenate([xr * hr - xi * hi, xr * hi + xi * hr], axis=0)


def _hyena_long_body(v_ref, x1_ref, x2_ref, g_ref, cw_ref, cb_ref, skip_ref, hf0_ref, hf1_ref,
                     ma_ref, fb_ref, fbi_ref, mc_ref, twc_ref, tws_ref, o_ref, y_ref, c_ref, a_ref, w_ref,
                     *, n1):
    n2 = DFT_INNER
    g = ROW_GROUP
    h1 = n1 // 2
    seq, cw = v_ref.shape
    half = n1 * g

    def long_conv(hf_ref):
        def emit(f1, z):
            hf = jnp.concatenate([hf_ref[f1 + j] for j in range(LANE_PAIR)], axis=1)
            spec = _complex_mul(_dot(fb_ref[...], z), hf, n2)
            w = _dot(fbi_ref[...], spec.astype(BF16))
            for j in range(LANE_PAIR):
                r0 = pl.multiple_of((f1 + j) * g, g)
                r1 = pl.multiple_of(half + (f1 + j) * g, g)
                w_ref[:, pl.ds(r0, g), :] = w[:n2, j * cw:(j + 1) * cw].reshape(n2 // g, g, cw)
                w_ref[:, pl.ds(r1, g), :] = w[n2:, j * cw:(j + 1) * cw].reshape(n2 // g, g, cw)

        _forward_two_stage(y_ref, a_ref, ma_ref, twc_ref, tws_ref, n1, emit)

        def stage_c(step, carry):
            his = [step * LANE_PAIR + j for j in range(LANE_PAIR)]
            cols = []
            for hi in his:
                res, ims = [], []
                twc = twc_ref[hi]
                tws = tws_ref[hi]
                for f1 in range(n1):
                    c = twc[f1:f1 + 1]
                    s = tws[f1:f1 + 1]
                    re = w_ref[hi, f1 * g:(f1 + 1) * g, :]
                    im = w_ref[hi, half + f1 * g:half + (f1 + 1) * g, :]
                    res.append(re * c + im * s)
                    ims.append(im * c - re * s)
                cols.append(jnp.concatenate(res + ims, axis=0).astype(BF16))
            y = _dot(mc_ref[...], jnp.concatenate(cols, axis=1))
            for j, hi in enumerate(his):
                c_ref[:, pl.ds(pl.multiple_of(hi * g, g), g), :] = y[:, j * cw:(j + 1) * cw].reshape(h1, g, cw)
            return carry

        lax.fori_loop(0, n2 // g // LANE_PAIR, stage_c, 0, unroll=2)

    def load3(ref):
        return ref[...].reshape(seq, cw)

    y_ref[...] = _short_conv(v_ref[...], cw_ref, cb_ref, 0).reshape(y_ref.shape)
    long_conv(hf0_ref)
    y = _short_conv(x1_ref[...], cw_ref, cb_ref, 1) * (load3(c_ref) + skip_ref[0:1, :] * load3(y_ref))
    y_ref[...] = y.reshape(y_ref.shape)
    long_conv(hf1_ref)
    out = _short_conv(x2_ref[...], cw_ref, cb_ref, 2) * (load3(c_ref) + skip_ref[1:2, :] * load3(y_ref))
    o_ref[...] = (out * _silu(g_ref[...])).astype(o_ref.dtype)


def _const_spec(shape):
    nd = len(shape)
    return pl.BlockSpec(shape, lambda *_: (0,) * nd, pipeline_mode=pl.Buffered(1))


def _hyena_long(p, conv_w, conv_b, skip, spectra, layer):
    b, seq, _ = p.shape
    cw = LANES
    n2 = DFT_INNER
    n1 = 2 * seq // n2
    nblk = HY_C // cw
    ma, fb, fbi, mc, twc, tws = _two_stage_tables(seq)

    def pcol(off):
        return pl.BlockSpec((None, seq, cw), lambda j, i: (i, 0, off // cw + j))

    def hf_spec(order):
        return pl.BlockSpec((None, n1, 2 * n2, cw), lambda j, i: (layer, 0, 0, order * nblk + j),
                            pipeline_mode=pl.Buffered(1))

    stage_rows = (n2 // ROW_GROUP, 2 * n1 * ROW_GROUP, cw)
    return pl.pallas_call(
        functools.partial(_hyena_long_body, n1=n1),
        grid=(nblk, b),
        in_specs=[
            pcol(OFF_HYZ), pcol(OFF_HYZ + HY_C), pcol(OFF_HYZ + 2 * HY_C), pcol(OFF_HYG),
            pl.BlockSpec((None, HY_ORDER + 1, SHORT_K, cw), lambda j, i: (layer, 0, 0, j)),
            pl.BlockSpec((None, HY_ORDER + 1, cw), lambda j, i: (layer, 0, j)),
            pl.BlockSpec((None, HY_ORDER, cw), lambda j, i: (layer, 0, j)),
            hf_spec(0), hf_spec(1),
            _const_spec(ma.shape), _const_spec(fb.shape), _const_spec(fbi.shape), _const_spec(mc.shape),
            _const_spec(twc.shape), _const_spec(tws.shape),
        ],
        out_specs=pl.BlockSpec((None, seq, cw), lambda j, i: (i, 0, j)),
        out_shape=jax.ShapeDtypeStruct((b, seq, HY_C), BF16),
        scratch_shapes=[
            pltpu.VMEM((n1 // 2, n2, cw), F32), pltpu.VMEM((n1 // 2, n2, cw), F32),
            pltpu.VMEM(stage_rows, F32), pltpu.VMEM(stage_rows, F32),
        ],
        compiler_params=_params("arbitrary", "arbitrary"),
        name="hyena_long",
    )(p, p, p, p, conv_w, conv_b, skip, spectra, spectra, ma, fb, fbi, mc, twc, tws)


def _hyena_short_body(v_ref, x1_ref, x2_ref, g_ref, cw_ref, cb_ref, skip_ref, hf0_ref, hf1_ref,
                      fwd_ref, inv_ref, o_ref):
    n = fwd_ref.shape[0] // 2

    def long_conv(y, hf_ref):
        spec = _complex_mul(_dot(fwd_ref[...], y.astype(BF16)), hf_ref[...], n)
        return _dot(inv_ref[...], spec.astype(BF16))

    y = _short_conv(v_ref[...], cw_ref, cb_ref, 0)
    y = _short_conv(x1_ref[...], cw_ref, cb_ref, 1) * (long_conv(y, hf0_ref) + skip_ref[0:1, :] * y)
    y = _short_conv(x2_ref[...], cw_ref, cb_ref, 2) * (long_conv(y, hf1_ref) + skip_ref[1:2, :] * y)
    o_ref[...] = (y * _silu(g_ref[...])).astype(o_ref.dtype)


def _hyena_short(p, conv_w, conv_b, skip, spectra, layer):
    b, seq, _ = p.shape
    cw = LANES
    nblk = HY_C // cw
    fwd, inv = _one_stage_tables(seq)

    def pcol(off):
        return pl.BlockSpec((None, seq, cw), lambda j, i: (i, 0, off // cw + j))

    def hf_spec(order):
        return pl.BlockSpec((None, 4 * seq, cw), lambda j, i: (layer, 0, order * nblk + j))

    return pl.pallas_call(
        _hyena_short_body,
        grid=(nblk, b),
        in_specs=[
            pcol(OFF_HYZ), pcol(OFF_HYZ + HY_C), pcol(OFF_HYZ + 2 * HY_C), pcol(OFF_HYG),
            pl.BlockSpec((None, HY_ORDER + 1, SHORT_K, cw), lambda j, i: (layer, 0, 0, j)),
            pl.BlockSpec((None, HY_ORDER + 1, cw), lambda j, i: (layer, 0, j)),
            pl.BlockSpec((None, HY_ORDER, cw), lambda j, i: (layer, 0, j)),
            hf_spec(0), hf_spec(1),
            pl.BlockSpec(fwd.shape, lambda j, i: (0, 0)), pl.BlockSpec(inv.shape, lambda j, i: (0, 0)),
        ],
        out_specs=pl.BlockSpec((None, seq, cw), lambda j, i: (i, 0, j)),
        out_shape=jax.ShapeDtypeStruct((b, seq, HY_C), BF16),
        compiler_params=_params("arbitrary", "arbitrary"),
        name="hyena_short",
    )(p, p, p, p, conv_w, conv_b, skip, spectra, spectra, fwd, inv)


def _chunk_mlp_body(u_ref, v_ref, g_ref, lg_ref, lb_ref, ws_ref, bs_ref, o_ref):
    v = v_ref[...]
    mu = jnp.mean(v, axis=-1, keepdims=True)
    vc = v - mu
    var = jnp.mean(vc * vc, axis=-1, keepdims=True)
    vn = (vc * lax.rsqrt(var + EPS) * lg_ref[...] + lb_ref[...]).astype(BF16)
    rows = v.shape[0]
    for n in range(rows // CHUNK):
        r = slice(n * CHUNK, (n + 1) * CHUNK)
        for h in range(CM_HEADS):
            c = slice(h * CM_HD, (h + 1) * CM_HD)
            s = _dot(ws_ref[h].astype(BF16), vn[r, c]) + bs_ref[:, c]
            o_ref[r, c] = (u_ref[r, c] * s * _silu(g_ref[r, c])).astype(o_ref.dtype)


def _chunk_mlp(p, ln_g, ln_b, w_s, b_s, layer):
    b, seq, _ = p.shape
    bm = min(seq, 512)
    bs_full = jnp.repeat(jnp.swapaxes(b_s, 1, 2), CM_HD, axis=2)

    def pcol(off):
        return pl.BlockSpec((None, bm, CM_C), lambda i, t: (i, t, off // CM_C))

    return pl.pallas_call(
        _chunk_mlp_body,
        grid=(b, seq // bm),
        in_specs=[
            pcol(OFF_CMU), pcol(OFF_CMV), pcol(OFF_CMG),
            pl.BlockSpec((None, 1, CM_C), lambda i, t: (layer, 0, 0)),
            pl.BlockSpec((None, 1, CM_C), lambda i, t: (layer, 0, 0)),
            pl.BlockSpec((None, CM_HEADS, CHUNK, CHUNK), lambda i, t: (layer, 0, 0, 0)),
            pl.BlockSpec((None, CHUNK, CM_C), lambda i, t: (layer, 0, 0)),
        ],
        out_specs=pl.BlockSpec((None, bm, CM_C), lambda i, t: (i, t, 0)),
        out_shape=jax.ShapeDtypeStruct((b, seq, CM_C), BF16),
        compiler_params=_params("arbitrary", "arbitrary"),
        name="chunk_mlp",
    )(p, p, p, ln_g.reshape(-1, 1, CM_C), ln_b.reshape(-1, 1, CM_C), w_s, bs_full)


def _rms_heads(x, g, nheads):
    outs = []
    for h in range(nheads):
        xh = x[:, h * HEAD_DIM:(h + 1) * HEAD_DIM]
        ms = jnp.mean(xh * xh, axis=-1, keepdims=True)
        outs.append(xh * lax.rsqrt(ms + EPS) * g)
    return outs


def _rope(xh, cos, sin):
    lane = lax.broadcasted_iota(jnp.int32, xh.shape, 1)
    quarter = ROPE_AXIS // 2
    partner = jnp.where((lane % ROPE_AXIS) < quarter,
                        pltpu.roll(xh, HEAD_DIM - quarter, 1), pltpu.roll(xh, quarter, 1))
    return xh * cos + partner * sin


def _kv_prep_body(k_ref, v_ref, gk_ref, cos_ref, sin_ref, kn_ref, kr_ref, vt_ref, *, rotary):
    heads = _rms_heads(k_ref[...], gk_ref[...], N_KV_HEADS)
    for h, kh in enumerate(heads):
        c = slice(h * HEAD_DIM, (h + 1) * HEAD_DIM)
        kn_ref[:, c] = kh
        if rotary:
            kh = _rope(kh, cos_ref[...], sin_ref[...])
        kr_ref[:, c] = kh.astype(BF16)
        vt_ref[h] = v_ref[:, c].T.astype(BF16)


def _kv_prep(p, k_norm_g, cos, sin, layer, rotary):
    b, seq, _ = p.shape
    bm = min(seq, 512)
    blk = pl.BlockSpec((None, bm, KV_C), lambda i, t: (i, t, 0))
    vt_blk = pl.BlockSpec((None, N_KV_HEADS, HEAD_DIM, bm), lambda i, t: (i, 0, 0, t))
    return pl.pallas_call(
        functools.partial(_kv_prep_body, rotary=rotary),
        grid=(b, seq // bm),
        in_specs=[
            pl.BlockSpec((None, bm, KV_C), lambda i, t: (i, t, OFF_K // KV_C)),
            pl.BlockSpec((None, bm, KV_C), lambda i, t: (i, t, OFF_V // KV_C)),
            pl.BlockSpec((None, 1, HEAD_DIM), lambda i, t: (layer, 0, 0)),
            pl.BlockSpec((bm, HEAD_DIM), lambda i, t: (t, 0)),
            pl.BlockSpec((bm, HEAD_DIM), lambda i, t: (t, 0)),
        ],
        out_specs=[blk, blk, vt_blk],
        out_shape=[jax.ShapeDtypeStruct((b, seq, KV_C), F32),
                   jax.ShapeDtypeStruct((b, seq, KV_C), BF16),
                   jax.ShapeDtypeStruct((b, N_KV_HEADS, HEAD_DIM, seq), BF16)],
        compiler_params=_params("arbitrary", "arbitrary"),
        name="kv_prep",
    )(p, p, k_norm_g.reshape(-1, 1, HEAD_DIM), cos, sin)


def _attention_body(*refs, rotary, has_ctx):
    q_ref, g_ref, gq_ref, cos_ref, sin_ref, k_ref, vt_ref = refs[:7]
    if has_ctx:
        ck_ref, cv_ref, o_ref, s_ref, p_ref = refs[7:]
    else:
        o_ref, s_ref, p_ref = refs[7:]
    tq = q_ref.shape[0]
    nq = GQA * tq
    s_self = k_ref.shape[0]
    s_all = s_ref.shape[0]
    chunk = min(ATT_KEY_CHUNK, s_self)
    sub = ROW_GROUP
    scale = math.log2(math.e) / math.sqrt(HEAD_DIM)
    heads = _rms_heads(q_ref[...], gq_ref[...], GQA)
    if rotary:
        heads = [_rope(qh, cos_ref[...], sin_ref[...]) for qh in heads]
    q = jnp.concatenate([(qh * scale).astype(BF16) for qh in heads], axis=0)
    contract_last = (((1,), (1,)), ((), ()))

    def col_reduce(x, op):
        return op(x.reshape(x.shape[0] // sub, sub, nq), axis=0)

    key_blocks = [slice(j * chunk, (j + 1) * chunk) for j in range(s_self // chunk)]
    m8 = None
    for rows in key_blocks:
        st = lax.dot_general(k_ref[rows, :], q, contract_last, preferred_element_type=F32)
        s_ref[rows, :] = st
        part = col_reduce(st, jnp.max)
        m8 = part if m8 is None else jnp.maximum(m8, part)
    if has_ctx:
        key_blocks.append(slice(s_self, s_all))
        st = lax.dot_general(ck_ref[...].astype(BF16), q, contract_last, preferred_element_type=F32)
        s_ref[s_self:, :] = st
        m8 = jnp.maximum(m8, col_reduce(st, jnp.max))
    m = jnp.max(m8, axis=0, keepdims=True)

    l8 = jnp.zeros((sub, nq), F32)
    for rows in key_blocks:
        e = jnp.exp2(s_ref[rows, :] - m)
        l8 = l8 + col_reduce(e, jnp.sum)
        p_ref[rows, :] = e.astype(BF16)
    den = jnp.sum(l8, axis=0, keepdims=True)

    ot = _dot(vt_ref[...], p_ref[0:s_self, :])
    if has_ctx:
        ot = ot + _dot(cv_ref[...].T.astype(BF16), p_ref[s_self:, :])
    ot = ot / den
    for h in range(GQA):
        c = slice(h * HEAD_DIM, (h + 1) * HEAD_DIM)
        oh = ot[:, h * tq:(h + 1) * tq].T
        o_ref[:, c] = (oh * _silu(g_ref[:, c])).astype(o_ref.dtype)


def _attention(p, q_norm_g, cos, sin, keys, values_t, ctx_k, ctx_v, layer, rotary, tq):
    b, seq, _ = p.shape
    skv = keys.shape[1]
    gw = GQA * HEAD_DIM
    has_ctx = ctx_k is not None
    in_specs = [
        pl.BlockSpec((None, tq, gw), lambda i, h, t: (i, t, OFF_Q // gw + h)),
        pl.BlockSpec((None, tq, gw), lambda i, h, t: (i, t, OFF_ATG // gw + h)),
        pl.BlockSpec((None, 1, HEAD_DIM), lambda i, h, t: (layer, 0, 0)),
        pl.BlockSpec((tq, HEAD_DIM), lambda i, h, t: (t, 0)),
        pl.BlockSpec((tq, HEAD_DIM), lambda i, h, t: (t, 0)),
        pl.BlockSpec((None, skv, HEAD_DIM), lambda i, h, t: (i, 0, h)),
        pl.BlockSpec((None, None, HEAD_DIM, skv), lambda i, h, t: (i, h, 0, 0)),
    ]
    args = [p, p, q_norm_g.reshape(-1, 1, HEAD_DIM), cos, sin, keys, values_t]
    s_all = skv
    if has_ctx:
        past = ctx_k.shape[2]
        spec = pl.BlockSpec((None, None, past, HEAD_DIM), lambda i, h, t: (i, layer, 0, h))
        in_specs += [spec, spec]
        args += [ctx_k, ctx_v]
        s_all += past
    return pl.pallas_call(
        functools.partial(_attention_body, rotary=rotary, has_ctx=has_ctx),
        grid=(b, N_KV_HEADS, seq // tq),
        in_specs=in_specs,
        out_specs=pl.BlockSpec((None, tq, gw), lambda i, h, t: (i, t, h)),
        out_shape=jax.ShapeDtypeStruct((b, seq, ATT_C), BF16),
        scratch_shapes=[pltpu.VMEM((s_all, GQA * tq), F32), pltpu.VMEM((s_all, GQA * tq), BF16)],
        compiler_params=_params("arbitrary", "arbitrary", "arbitrary"),
        name="attention",
    )(*args)


def _out_proj_body(yh_ref, yc_ref, ya_ref, x_ref, mod_ref, w_ref, lg_ref, lb_ref, o_ref):
    d = x_ref.shape[-1]
    out = _dot(yh_ref[...], w_ref[0:HY_C, :])
    out = out + _dot(yc_ref[...], w_ref[HY_C:HY_C + CM_C, :])
    out = out + _dot(ya_ref[...], w_ref[HY_C + CM_C:D_MIX, :])
    gate = mod_ref[:, 2 * d:3 * d]
    r = DN_ALPHA * x_ref[...] + gate * out
    mu = jnp.mean(r, axis=-1, keepdims=True)
    rc = r - mu
    var = jnp.mean(rc * rc, axis=-1, keepdims=True)
    o_ref[...] = rc * lax.rsqrt(var + EPS) * lg_ref[...] + lb_ref[...]


def _out_proj(y_hy, y_cm, y_at, x, mod, row0, w_out_bf, ln_g, ln_b, layer):
    g, t, d = x.shape
    bm = 512

    def rows(c):
        return pl.BlockSpec((None, bm, c), lambda b, i: (b, i, 0))

    return pl.pallas_call(
        _out_proj_body,
        grid=(g, t // bm),
        in_specs=[
            rows(HY_C), rows(CM_C), rows(ATT_C), rows(d),
            pl.BlockSpec((None, None, 1, 3 * d), lambda b, i: (layer, row0 + b, 0, 0)),
            pl.BlockSpec((None, D_MIX, d), lambda b, i: (layer, 0, 0)),
            pl.BlockSpec((None, 1, d), lambda b, i: (layer, 0, 0)),
            pl.BlockSpec((None, 1, d), lambda b, i: (layer, 0, 0)),
        ],
        out_specs=rows(d),
        out_shape=jax.ShapeDtypeStruct((g, t, d), F32),
        compiler_params=_params("arbitrary", "arbitrary"),
        name="out_proj_ln",
    )(y_hy, y_cm, y_at, x, mod, w_out_bf, ln_g.reshape(-1, 1, d), ln_b.reshape(-1, 1, d))


@functools.lru_cache(maxsize=None)
def _rope_tables(seq):
    rows = seq // GRID_W
    row = np.repeat(np.arange(rows, dtype=np.float32), GRID_W)
    col = np.tile(np.arange(GRID_W, dtype=np.float32), rows)
    inv = (np.float32(ROPE_THETA) ** (-np.arange(0, ROPE_AXIS, 2, dtype=np.float32) / np.float32(ROPE_AXIS))
           ).astype(np.float32)
    ra = row[:, None] * inv[None, :]
    ca = col[:, None] * inv[None, :]
    cos = np.concatenate([np.cos(ra), np.cos(ra), np.cos(ca), np.cos(ca)], axis=1)
    sin = np.concatenate([-np.sin(ra), np.sin(ra), -np.sin(ca), np.sin(ca)], axis=1)
    return jnp.asarray(cos, F32), jnp.asarray(sin, F32)


def _mixer_layer(x, seq, mod, row0, layer, rotary, w_in_bf, w_out_bf, spectra, hy_conv_w, hy_conv_b, hy_skip,
                 cm_ln_g, cm_ln_b, cm_w_s, cm_b_s, q_norm_g, k_norm_g, ln_g, ln_b, ctx_k, ctx_v, cos, sin):
    g, t, d = x.shape
    nreq = g * t // seq
    p = _in_proj(x, mod, row0, w_in_bf, layer).reshape(nreq, seq, D_IN)
    if rotary:
        y_hy = _hyena_long(p, hy_conv_w, hy_conv_b, hy_skip, spectra, layer)
    else:
        y_hy = _hyena_short(p, hy_conv_w, hy_conv_b, hy_skip, spectra, layer)
    y_cm = _chunk_mlp(p, cm_ln_g, cm_ln_b, cm_w_s, cm_b_s, layer)
    k_normed, keys, values = _kv_prep(p, k_norm_g, cos, sin, layer, rotary)
    y_at = _attention(p, q_norm_g, cos, sin, keys, values, ctx_k, ctx_v, layer, rotary,
                      tq=min(seq, 128))
    x_new = _out_proj(y_hy.reshape(g, t, HY_C), y_cm.reshape(g, t, CM_C), y_at.reshape(g, t, ATT_C),
                      x, mod, row0, w_out_bf, ln_g, ln_b, layer)
    return x_new, k_normed, p


def kernel(x_prompt, x_sample, cache_k, cache_v, c, c_ctx, w_mod, b_mod, w_in, hy_conv_w, hy_conv_b, hy_f_w1, hy_f_b1, hy_f_w2, hy_f_b2, hy_f_w3, hy_f_freq, hy_skip, cm_ln_g, cm_ln_b, cm_w_s, cm_b_s, q_norm_g, k_norm_g, w_out, ln_g, ln_b):
    batch, seq, d = x_prompt.shape
    dec_batch, dec_seq, _ = x_sample.shape
    depth = w_in.shape[0]
    past = cache_k.shape[2]

    cond = jnp.concatenate([c_ctx[None, :], c, jnp.zeros((MOD_ROWS - 1 - dec_batch, d), F32)], axis=0)
    mod = _modulation(cond, w_mod, b_mod).reshape(depth, MOD_ROWS, 1, 3 * d)

    w_in_bf = w_in.astype(BF16)
    w_out_bf = w_out.astype(BF16)
    fweights = _filter_weights(hy_f_w1, hy_f_b1, hy_f_w2, hy_f_b2, hy_f_w3, hy_f_freq)
    spectra_ctx = _filter_spectra_one_stage(seq, fweights)
    spectra_lat = _filter_spectra_two_stage(dec_seq, fweights)
    conv_w = hy_conv_w.reshape(depth, SHORT_K, HY_ORDER + 1, HY_C).transpose(0, 2, 1, 3)
    conv_b = hy_conv_b.reshape(depth, HY_ORDER + 1, HY_C)
    cos, sin = _rope_tables(dec_seq)
    ctx_k = cache_k.reshape(dec_batch, depth, past, KV_C)
    ctx_v = cache_v.reshape(dec_batch, depth, past, KV_C)

    xp = x_prompt.reshape(1, batch * seq, d)
    xs = x_sample
    new_k, new_v = [], []
    for layer in range(depth):
        common = dict(layer=layer, w_in_bf=w_in_bf, w_out_bf=w_out_bf, hy_conv_w=conv_w, hy_conv_b=conv_b,
                      hy_skip=hy_skip, cm_ln_g=cm_ln_g, cm_ln_b=cm_ln_b, cm_w_s=cm_w_s, cm_b_s=cm_b_s,
                      q_norm_g=q_norm_g, k_norm_g=k_norm_g, ln_g=ln_g, ln_b=ln_b, cos=cos, sin=sin)
        xp, k_l, p_ctx = _mixer_layer(xp, seq, mod, 0, rotary=False, spectra=spectra_ctx,
                                      ctx_k=None, ctx_v=None, **common)
        new_k.append(k_l.reshape(batch, seq, N_KV_HEADS, HEAD_DIM))
        new_v.append(p_ctx[:, :, OFF_V:OFF_V + KV_C].reshape(batch, seq, N_KV_HEADS, HEAD_DIM))
        xs, _, _ = _mixer_layer(xs, dec_seq, mod, 1, rotary=True, spectra=spectra_lat,
                                ctx_k=ctx_k, ctx_v=ctx_v, **common)
    return (xp.reshape(batch, seq, d), xs, jnp.stack(new_k, axis=1), jnp.stack(new_v, axis=1))
```

```python
import functools
import math

import numpy as np
import jax
import jax.numpy as jnp
from jax import lax
from jax.experimental import pallas as pl
from jax.experimental.pallas import tpu as pltpu

F32 = jnp.float32
BF16 = jnp.bfloat16

D_MODEL = 2048
DEPTH = 4
GRID_W = 64
HY_C = 512
HY_ORDER = 2
SHORT_K = 3
FILT_EMB = 33
FILT_HID = 64
HY_FAST_DECAY = 0.3
HY_SLOW_DECAY = 1.5
HY_TARGET = 1e-2
CM_HEADS = 4
CM_HD = 128
CM_C = CM_HEADS * CM_HD
CHUNK = 128
N_HEADS = 8
N_KV_HEADS = 2
HEAD_DIM = 128
GQA = N_HEADS // N_KV_HEADS
ATT_C = N_HEADS * HEAD_DIM
KV_C = N_KV_HEADS * HEAD_DIM
ROPE_THETA = 10000.0
ROPE_AXIS = HEAD_DIM // 2
D_MIX = HY_C + CM_C + ATT_C
EPS = 1e-6
DN_ALPHA = (2 * DEPTH) ** 0.25

OFF_HYZ = 0
OFF_HYG = (HY_ORDER + 1) * HY_C
OFF_CMU = OFF_HYG + HY_C
OFF_CMV = OFF_CMU + CM_C
OFF_CMG = OFF_CMV + CM_C
OFF_Q = OFF_CMG + CM_C
OFF_K = OFF_Q + ATT_C
OFF_V = OFF_K + KV_C
OFF_ATG = OFF_V + KV_C
D_IN = OFF_ATG + ATT_C

LANES = 128
VMEM_LIMIT_BYTES = 56 * 1024 * 1024

DFT_INNER = 128
ROW_GROUP = 8
LANE_PAIR = 2
ATT_KEY_CHUNK = 512
ATT_SUBTILE_ROWS = 128
ATT_ONES_ROWS = 16
MOD_ROWS = 16


def _params(*semantics):
    return pltpu.CompilerParams(dimension_semantics=semantics, vmem_limit_bytes=VMEM_LIMIT_BYTES)


def _silu(x):
    return x * jax.nn.sigmoid(x)


def _dot(a, b):
    return jnp.dot(a, b, preferred_element_type=F32)


def _dot_exact(a, b):
    return jnp.dot(a, b, preferred_element_type=F32, precision=lax.Precision.HIGHEST)


def _mod_body(c_ref, w_ref, b_ref, o_ref):
    s = _silu(c_ref[...]).astype(BF16)
    o_ref[...] = _dot(s, w_ref[...].astype(BF16)) + b_ref[...]


def _modulation(cond, w_mod, b_mod):
    depth, d, d3 = w_mod.shape
    bn = 1536
    return pl.pallas_call(
        _mod_body,
        grid=(depth, d3 // bn),
        in_specs=[
            pl.BlockSpec((MOD_ROWS, d), lambda l, j: (0, 0)),
            pl.BlockSpec((None, d, bn), lambda l, j: (l, 0, j)),
            pl.BlockSpec((None, 1, bn), lambda l, j: (l, 0, j)),
        ],
        out_specs=pl.BlockSpec((None, MOD_ROWS, bn), lambda l, j: (l, 0, j)),
        out_shape=jax.ShapeDtypeStruct((depth, MOD_ROWS, d3), F32),
        compiler_params=_params("arbitrary", "arbitrary"),
        name="modulation",
    )(cond, w_mod, b_mod.reshape(depth, 1, d3))


def _in_proj_body(x_ref, mod_ref, w_ref, o_ref):
    d = x_ref.shape[-1]
    shift = mod_ref[:, 0:d]
    scale = mod_ref[:, d:2 * d]
    h = (x_ref[...] * (1.0 + scale) + shift).astype(BF16)
    o_ref[...] = _dot(h, w_ref[...])


def _in_proj(x, mod, row0, w_in_bf, layer):
    g, t, d = x.shape
    d_in = w_in_bf.shape[-1]
    bm, bn = 1024, 1024
    return pl.pallas_call(
        _in_proj_body,
        grid=(g, t // bm, d_in // bn),
        in_specs=[
            pl.BlockSpec((None, bm, d), lambda b, i, j: (b, i, 0)),
            pl.BlockSpec((None, None, 1, 3 * d), lambda b, i, j: (layer, row0 + b, 0, 0)),
            pl.BlockSpec((None, d, bn), lambda b, i, j: (layer, 0, j)),
        ],
        out_specs=pl.BlockSpec((None, bm, bn), lambda b, i, j: (b, i, j)),
        out_shape=jax.ShapeDtypeStruct((g, t, d_in), F32),
        compiler_params=_params("arbitrary", "arbitrary", "arbitrary"),
        name="in_proj",
    )(x, mod, w_in_bf)


def _cs(k, n):
    ang = 2.0 * np.pi * (np.asarray(k, dtype=np.int64) % n).astype(np.float64) / n
    return np.cos(ang), np.sin(ang)


@functools.lru_cache(maxsize=None)
def _two_stage_tables(seq):
    n = 2 * seq
    n2 = DFT_INNER
    n1 = n // n2
    h1 = n1 // 2
    g = ROW_GROUP
    nf = -(-(n1 // 2 + 1) // LANE_PAIR) * LANE_PAIR
    t1 = np.arange(h1)
    f1 = np.arange(nf)
    live = (f1 <= n1 // 2).astype(np.float64)
    mirror_weight = np.where((f1 == 0) | (f1 == n1 // 2), 1.0, 2.0) * live
    t2 = np.arange(n2)
    f2 = np.arange(n2)
    lo = np.arange(g)
    eye = np.eye(g)
    k = t1[None, None, :, None] * f1[:, None, None, None] * n2 + lo[None, None, None, :] * f1[:, None, None, None]
    c, s = _cs(k, n)
    c = c * eye[None, :, None, :] * live[:, None, None, None]
    s = s * eye[None, :, None, :] * live[:, None, None, None]
    ma = np.concatenate([c.reshape(nf * g, h1 * g), -s.reshape(nf * g, h1 * g)], axis=0)
    hi = np.arange(n2 // g)
    c, s = _cs(g * hi[:, None] * f1[None, :], n)
    twc = np.broadcast_to(c[:, :, None], c.shape + (LANES,))
    tws = np.broadcast_to(-s[:, :, None], s.shape + (LANES,))
    c, s = _cs(f2[:, None] * t2[None, :], n2)
    fb = np.block([[c, s], [-s, c]])
    c, s = _cs(t2[:, None] * f2[None, :], n2)
    fbi = np.block([[c, -s], [s, c]])
    t1o = np.arange(h1) + h1 // 2
    k = t1o[:, None, None, None] * f1[None, None, :, None] * n2 + lo[None, None, None, :] * f1[None, None, :, None]
    c, s = _cs(k, n)
    c = c * eye[None, :, None, :] * mirror_weight[None, None, :, None]
    s = s * eye[None, :, None, :] * mirror_weight[None, None, :, None]
    mc = np.concatenate([c.reshape(h1 * g, nf * g), -s.reshape(h1 * g, nf * g)], axis=1)
    return (jnp.asarray(ma, BF16), jnp.asarray(fb, BF16), jnp.asarray(fbi, BF16), jnp.asarray(mc, BF16),
            jnp.asarray(twc, F32), jnp.asarray(tws, F32))


@functools.lru_cache(maxsize=None)
def _one_stage_tables(seq):
    n = 2 * seq
    f = np.arange(n)
    t = np.arange(seq)
    c, s = _cs(f[:, None] * t[None, :], n)
    fwd = np.concatenate([c, -s], axis=0)
    to = np.arange(seq) + seq // 2
    c, s = _cs(to[:, None] * f[None, :], n)
    inv = np.concatenate([c, -s], axis=1)
    return jnp.asarray(fwd, BF16), jnp.asarray(inv, BF16)


@functools.lru_cache(maxsize=None)
def _filter_constants(seq):
    pos = np.arange(seq, dtype=np.float32)
    t = pos / np.float32(max(seq - 1, 1))
    bands = (FILT_EMB - 1) // 2
    fb = np.linspace(1e-4, bands - 1, bands, dtype=np.float32)
    ang = np.float32(2.0 * math.pi / seq) * pos[:, None] * fb[None, :]
    feats = np.zeros((seq, FILT_HID), np.float32)
    feats[:, 0] = t
    feats[:, 1:1 + bands] = np.cos(ang)
    feats[:, 1 + bands:FILT_EMB] = -np.sin(ang)
    dist = np.abs(pos - (seq // 2)) / np.float32(seq / 2.0)
    dist = np.broadcast_to(dist[:, None], (seq, LANES)).astype(np.float32)
    deltas = np.abs(np.linspace(math.log(HY_TARGET) / HY_SLOW_DECAY, math.log(HY_TARGET) / HY_FAST_DECAY,
                                HY_C, dtype=np.float32))
    deltas = np.tile(deltas, HY_ORDER)[None, :]
    return jnp.asarray(feats), jnp.asarray(dist), jnp.asarray(deltas)


def _filter_hidden_body(feats_ref, w1_ref, b1_ref, w2_ref, b2_ref, fr_ref, o_ref):
    fr = fr_ref[...]
    h = jnp.sin(fr * (_dot_exact(feats_ref[...], w1_ref[...]) + b1_ref[...]))
    o_ref[...] = jnp.sin(fr * (_dot_exact(h, w2_ref[...]) + b2_ref[...]))


def _filter_hidden(seq, fweights):
    w1, b1, w2, b2, _, freq = fweights
    depth = w1.shape[0]
    feats, _, _ = _filter_constants(seq)
    mat = pl.BlockSpec((None, FILT_HID, FILT_HID), lambda l: (l, 0, 0))
    vec = pl.BlockSpec((None, 1, FILT_HID), lambda l: (l, 0, 0))
    return pl.pallas_call(
        _filter_hidden_body,
        grid=(depth,),
        in_specs=[pl.BlockSpec((seq, FILT_HID), lambda l: (0, 0)), mat, vec, mat, vec, vec],
        out_specs=pl.BlockSpec((None, seq, FILT_HID), lambda l: (l, 0, 0)),
        out_shape=jax.ShapeDtypeStruct((depth, seq, FILT_HID), F32),
        compiler_params=_params("arbitrary"),
        name="hyena_filter_hidden",
    )(feats, w1, b1, w2, b2, freq)


def _implicit_filter(hid_ref, dist_ref, delt_ref, w3_ref):
    return _dot_exact(hid_ref[...], w3_ref[...]) * jnp.exp(-dist_ref[...] * delt_ref[...])


def _forward_two_stage(src_ref, a_ref, ma_ref, twc_ref, tws_ref, n1, emit):
    n2 = DFT_INNER
    g = ROW_GROUP
    h1 = n1 // 2
    nf = twc_ref.shape[1]
    cw = src_ref.shape[-1]
    half = nf * g

    def stage_a(step, carry):
        his = [step * LANE_PAIR + j for j in range(LANE_PAIR)]
        xs = [src_ref[:, pl.ds(pl.multiple_of(hi * g, g), g), :].reshape(h1 * g, cw).astype(BF16) for hi in his]
        out = _dot(ma_ref[...], jnp.concatenate(xs, axis=1))
        for j, hi in enumerate(his):
            lanes = slice(j * cw, (j + 1) * cw)
            twc = twc_ref[hi]
            tws = tws_ref[hi]
            for f1 in range(nf):
                c = twc[f1:f1 + 1]
                s = tws[f1:f1 + 1]
                re = out[f1 * g:(f1 + 1) * g, lanes]
                im = out[half + f1 * g:half + (f1 + 1) * g, lanes]
                a_ref[hi, f1 * g:(f1 + 1) * g, :] = re * c - im * s
                a_ref[hi, half + f1 * g:half + (f1 + 1) * g, :] = re * s + im * c
        return carry

    lax.fori_loop(0, n2 // g // LANE_PAIR, stage_a, 0, unroll=2)

    for f0 in range(0, nf, LANE_PAIR):
        zs = []
        for f1 in range(f0, f0 + LANE_PAIR):
            ar = a_ref[:, f1 * g:(f1 + 1) * g, :].reshape(n2, cw)
            ai = a_ref[:, half + f1 * g:half + (f1 + 1) * g, :].reshape(n2, cw)
            zs.append(jnp.concatenate([ar, ai], axis=0).astype(BF16))
        emit(f0, jnp.concatenate(zs, axis=1))


def _filter_two_stage_body(hid_ref, dist_ref, delt_ref, w3_ref, ma_ref, fb_ref, twc_ref, tws_ref,
                           o_ref, h_ref, a_ref, *, n1):
    h = _implicit_filter(hid_ref, dist_ref, delt_ref, w3_ref)
    h_ref[...] = h.reshape(h_ref.shape)
    inv_n = 1.0 / (n1 * DFT_INNER)

    cw = h_ref.shape[-1]

    def emit(f1, z):
        spec = (_dot(fb_ref[...], z) * inv_n).astype(o_ref.dtype)
        for j in range(LANE_PAIR):
            o_ref[f1 + j] = spec[:, j * cw:(j + 1) * cw]

    _forward_two_stage(h_ref, a_ref, ma_ref, twc_ref, tws_ref, n1, emit)


def _filter_one_stage_body(hid_ref, dist_ref, delt_ref, w3_ref, fwd_ref, o_ref):
    h = _implicit_filter(hid_ref, dist_ref, delt_ref, w3_ref)
    inv_n = 1.0 / fwd_ref.shape[0] * 2.0
    o_ref[...] = (_dot(fwd_ref[...], h.astype(BF16)) * inv_n).astype(o_ref.dtype)


def _filter_input_specs(seq, cw):
    return [
        pl.BlockSpec((None, seq, FILT_HID), lambda l, j: (l, 0, 0)),
        pl.BlockSpec((seq, LANES), lambda l, j: (0, 0)),
        pl.BlockSpec((1, cw), lambda l, j: (0, j)),
        pl.BlockSpec((None, FILT_HID, cw), lambda l, j: (l, 0, j)),
    ]


def _filter_weights(hy_f_w1, hy_f_b1, hy_f_w2, hy_f_b2, hy_f_w3, hy_f_freq):
    depth = hy_f_w1.shape[0]
    w1 = jnp.pad(hy_f_w1, ((0, 0), (0, FILT_HID - FILT_EMB), (0, 0)))
    return (w1, hy_f_b1.reshape(depth, 1, FILT_HID), hy_f_w2, hy_f_b2.reshape(depth, 1, FILT_HID),
            hy_f_w3, hy_f_freq.reshape(depth, 1, FILT_HID))


def _filter_spectra_two_stage(seq, fweights):
    depth = fweights[0].shape[0]
    cw = LANES
    n2 = DFT_INNER
    n1 = 2 * seq // n2
    _, dist, deltas = _filter_constants(seq)
    ma, fb, _, _, twc, tws = _two_stage_tables(seq)
    nf = twc.shape[1]
    ncol = HY_ORDER * HY_C
    return pl.pallas_call(
        functools.partial(_filter_two_stage_body, n1=n1),
        grid=(depth, ncol // cw),
        in_specs=[
            *_filter_input_specs(seq, cw),
            pl.BlockSpec(ma.shape, lambda l, j: (0, 0)),
            pl.BlockSpec(fb.shape, lambda l, j: (0, 0)),
            pl.BlockSpec(twc.shape, lambda l, j: (0, 0, 0)),
            pl.BlockSpec(tws.shape, lambda l, j: (0, 0, 0)),
        ],
        out_specs=pl.BlockSpec((None, nf, 2 * n2, cw), lambda l, j: (l, 0, 0, j)),
        out_shape=jax.ShapeDtypeStruct((depth, nf, 2 * n2, ncol), BF16),
        scratch_shapes=[pltpu.VMEM((n1 // 2, n2, cw), F32),
                        pltpu.VMEM((n2 // ROW_GROUP, 2 * nf * ROW_GROUP, cw), F32)],
        compiler_params=_params("arbitrary", "arbitrary"),
        name="hyena_filter_long",
    )(_filter_hidden(seq, fweights), dist, deltas, fweights[4], ma, fb, twc, tws)


def _filter_spectra_one_stage(seq, fweights):
    depth = fweights[0].shape[0]
    cw = LANES
    _, dist, deltas = _filter_constants(seq)
    fwd, _ = _one_stage_tables(seq)
    ncol = HY_ORDER * HY_C
    return pl.pallas_call(
        _filter_one_stage_body,
        grid=(depth, ncol // cw),
        in_specs=[*_filter_input_specs(seq, cw), pl.BlockSpec(fwd.shape, lambda l, j: (0, 0))],
        out_specs=pl.BlockSpec((None, 4 * seq, cw), lambda l, j: (l, 0, j)),
        out_shape=jax.ShapeDtypeStruct((depth, 4 * seq, ncol), BF16),
        compiler_params=_params("arbitrary", "arbitrary"),
        name="hyena_filter_short",
    )(_filter_hidden(seq, fweights), dist, deltas, fweights[4], fwd)


def _short_conv(x, w_ref, b_ref, part):
    seq = x.shape[0]
    rows = lax.broadcasted_iota(jnp.int32, x.shape, 0)
    prev = jnp.where(rows == 0, 0.0, pltpu.roll(x, 1, 0))
    nxt = jnp.where(rows == seq - 1, 0.0, pltpu.roll(x, seq - 1, 0))
    w = w_ref[part]
    return prev * w[0:1] + x * w[1:2] + nxt * w[2:3] + b_ref[part:part + 1, :]


def _complex_mul(x, h, half):
    xr, xi = x[:half], x[half:]
    hr, hi = h[:half].astype(F32), h[half:].astype(F32)
    return jnp.concatenate([xr * hr - xi * hi, xr * hi + xi * hr], axis=0)


def _hyena_long_body(v_ref, x1_ref, x2_ref, g_ref, cw_ref, cb_ref, skip_ref, hf0_ref, hf1_ref,
                     ma_ref, fb_ref, fbi_ref, mc_ref, twc_ref, tws_ref, o_ref, y_ref, c_ref, a_ref, w_ref,
                     *, n1):
    n2 = DFT_INNER
    g = ROW_GROUP
    h1 = n1 // 2
    seq, cw = v_ref.shape
    nf = twc_ref.shape[1]
    half = nf * g

    def long_conv(hf_ref):
        def emit(f1, z):
            hf = jnp.concatenate([hf_ref[f1 + j] for j in range(LANE_PAIR)], axis=1)
            spec = _complex_mul(_dot(fb_ref[...], z), hf, n2)
            w = _dot(fbi_ref[...], spec.astype(BF16))
            for j in range(LANE_PAIR):
                r0 = (f1 + j) * g
                r1 = half + (f1 + j) * g
                w_ref[:, r0:r0 + g, :] = w[:n2, j * cw:(j + 1) * cw].reshape(n2 // g, g, cw)
                w_ref[:, r1:r1 + g, :] = w[n2:, j * cw:(j + 1) * cw].reshape(n2 // g, g, cw)

        _forward_two_stage(y_ref, a_ref, ma_ref, twc_ref, tws_ref, n1, emit)

        def stage_c(step, carry):
            his = [step * LANE_PAIR + j for j in range(LANE_PAIR)]
            cols = []
            for hi in his:
                res, ims = [], []
                twc = twc_ref[hi]
                tws = tws_ref[hi]
                for f1 in range(nf):
                    c = twc[f1:f1 + 1]
                    s = tws[f1:f1 + 1]
                    re = w_ref[hi, f1 * g:(f1 + 1) * g, :]
                    im = w_ref[hi, half + f1 * g:half + (f1 + 1) * g, :]
                    res.append(re * c + im * s)
                    ims.append(im * c - re * s)
                cols.append(jnp.concatenate(res + ims, axis=0).astype(BF16))
            y = _dot(mc_ref[...], jnp.concatenate(cols, axis=1))
            for j, hi in enumerate(his):
                c_ref[:, pl.ds(pl.multiple_of(hi * g, g), g), :] = y[:, j * cw:(j + 1) * cw].reshape(h1, g, cw)
            return carry

        lax.fori_loop(0, n2 // g // LANE_PAIR, stage_c, 0, unroll=2)

    def load3(ref):
        return ref[...].reshape(seq, cw)

    y_ref[...] = _short_conv(v_ref[...], cw_ref, cb_ref, 0).reshape(y_ref.shape)
    long_conv(hf0_ref)
    y = _short_conv(x1_ref[...], cw_ref, cb_ref, 1) * (load3(c_ref) + skip_ref[0:1, :] * load3(y_ref))
    y_ref[...] = y.reshape(y_ref.shape)
    long_conv(hf1_ref)
    out = _short_conv(x2_ref[...], cw_ref, cb_ref, 2) * (load3(c_ref) + skip_ref[1:2, :] * load3(y_ref))
    o_ref[...] = (out * _silu(g_ref[...])).astype(o_ref.dtype)


def _const_spec(shape):
    nd = len(shape)
    return pl.BlockSpec(shape, lambda *_: (0,) * nd, pipeline_mode=pl.Buffered(1))


def _hyena_long(p, conv_w, conv_b, skip, spectra, layer):
    b, seq, _ = p.shape
    cw = LANES
    n2 = DFT_INNER
    n1 = 2 * seq // n2
    nblk = HY_C // cw
    ma, fb, fbi, mc, twc, tws = _two_stage_tables(seq)
    nf = twc.shape[1]

    def pcol(off):
        return pl.BlockSpec((None, seq, cw), lambda j, i: (i, 0, off // cw + j))

    def hf_spec(order):
        return pl.BlockSpec((None, nf, 2 * n2, cw), lambda j, i: (layer, 0, 0, order * nblk + j),
                            pipeline_mode=pl.Buffered(1))

    stage_rows = (n2 // ROW_GROUP, 2 * nf * ROW_GROUP, cw)
    return pl.pallas_call(
        functools.partial(_hyena_long_body, n1=n1),
        grid=(nblk, b),
        in_specs=[
            pcol(OFF_HYZ), pcol(OFF_HYZ + HY_C), pcol(OFF_HYZ + 2 * HY_C), pcol(OFF_HYG),
            pl.BlockSpec((None, HY_ORDER + 1, SHORT_K, cw), lambda j, i: (layer, 0, 0, j)),
            pl.BlockSpec((None, HY_ORDER + 1, cw), lambda j, i: (layer, 0, j)),
            pl.BlockSpec((None, HY_ORDER, cw), lambda j, i: (layer, 0, j)),
            hf_spec(0), hf_spec(1),
            _const_spec(ma.shape), _const_spec(fb.shape), _const_spec(fbi.shape), _const_spec(mc.shape),
            _const_spec(twc.shape), _const_spec(tws.shape),
        ],
        out_specs=pl.BlockSpec((None, seq, cw), lambda j, i: (i, 0, j)),
        out_shape=jax.ShapeDtypeStruct((b, seq, HY_C), BF16),
        scratch_shapes=[
            pltpu.VMEM((n1 // 2, n2, cw), F32), pltpu.VMEM((n1 // 2, n2, cw), F32),
            pltpu.VMEM(stage_rows, F32), pltpu.VMEM(stage_rows, F32),
        ],
        compiler_params=_params("arbitrary", "arbitrary"),
        name="hyena_long",
    )(p, p, p, p, conv_w, conv_b, skip, spectra, spectra, ma, fb, fbi, mc, twc, tws)


def _hyena_short_body(v_ref, x1_ref, x2_ref, g_ref, cw_ref, cb_ref, skip_ref, hf0_ref, hf1_ref,
                      fwd_ref, inv_ref, o_ref):
    n = fwd_ref.shape[0] // 2

    def long_conv(y, hf_ref):
        spec = _complex_mul(_dot(fwd_ref[...], y.astype(BF16)), hf_ref[...], n)
        return _dot(inv_ref[...], spec.astype(BF16))

    y = _short_conv(v_ref[...], cw_ref, cb_ref, 0)
    y = _short_conv(x1_ref[...], cw_ref, cb_ref, 1) * (long_conv(y, hf0_ref) + skip_ref[0:1, :] * y)
    y = _short_conv(x2_ref[...], cw_ref, cb_ref, 2) * (long_conv(y, hf1_ref) + skip_ref[1:2, :] * y)
    o_ref[...] = (y * _silu(g_ref[...])).astype(o_ref.dtype)


def _hyena_short(p, conv_w, conv_b, skip, spectra, layer):
    b, seq, _ = p.shape
    cw = LANES
    nblk = HY_C // cw
    fwd, inv = _one_stage_tables(seq)

    def pcol(off):
        return pl.BlockSpec((None, seq, cw), lambda j, i: (i, 0, off // cw + j))

    def hf_spec(order):
        return pl.BlockSpec((None, 4 * seq, cw), lambda j, i: (layer, 0, order * nblk + j))

    return pl.pallas_call(
        _hyena_short_body,
        grid=(nblk, b),
        in_specs=[
            pcol(OFF_HYZ), pcol(OFF_HYZ + HY_C), pcol(OFF_HYZ + 2 * HY_C), pcol(OFF_HYG),
            pl.BlockSpec((None, HY_ORDER + 1, SHORT_K, cw), lambda j, i: (layer, 0, 0, j)),
            pl.BlockSpec((None, HY_ORDER + 1, cw), lambda j, i: (layer, 0, j)),
            pl.BlockSpec((None, HY_ORDER, cw), lambda j, i: (layer, 0, j)),
            hf_spec(0), hf_spec(1),
            pl.BlockSpec(fwd.shape, lambda j, i: (0, 0)), pl.BlockSpec(inv.shape, lambda j, i: (0, 0)),
        ],
        out_specs=pl.BlockSpec((None, seq, cw), lambda j, i: (i, 0, j)),
        out_shape=jax.ShapeDtypeStruct((b, seq, HY_C), BF16),
        compiler_params=_params("arbitrary", "arbitrary"),
        name="hyena_short",
    )(p, p, p, p, conv_w, conv_b, skip, spectra, spectra, fwd, inv)


def _chunk_mlp_body(u_ref, v_ref, g_ref, lg_ref, lb_ref, ws_ref, bs_ref, o_ref):
    v = v_ref[...]
    mu = jnp.mean(v, axis=-1, keepdims=True)
    vc = v - mu
    var = jnp.mean(vc * vc, axis=-1, keepdims=True)
    vn = (vc * lax.rsqrt(var + EPS) * lg_ref[...] + lb_ref[...]).astype(BF16)
    rows = v.shape[0]
    for n in range(rows // CHUNK):
        r = slice(n * CHUNK, (n + 1) * CHUNK)
        for h in range(CM_HEADS):
            c = slice(h * CM_HD, (h + 1) * CM_HD)
            s = _dot(ws_ref[h].astype(BF16), vn[r, c]) + bs_ref[:, c]
            o_ref[r, c] = (u_ref[r, c] * s * _silu(g_ref[r, c])).astype(o_ref.dtype)


def _chunk_mlp(p, ln_g, ln_b, w_s, b_s, layer):
    b, seq, _ = p.shape
    bm = min(seq, 512)
    bs_full = jnp.repeat(jnp.swapaxes(b_s, 1, 2), CM_HD, axis=2)

    def pcol(off):
        return pl.BlockSpec((None, bm, CM_C), lambda i, t: (i, t, off // CM_C))

    return pl.pallas_call(
        _chunk_mlp_body,
        grid=(b, seq // bm),
        in_specs=[
            pcol(OFF_CMU), pcol(OFF_CMV), pcol(OFF_CMG),
            pl.BlockSpec((None, 1, CM_C), lambda i, t: (layer, 0, 0)),
            pl.BlockSpec((None, 1, CM_C), lambda i, t: (layer, 0, 0)),
            pl.BlockSpec((None, CM_HEADS, CHUNK, CHUNK), lambda i, t: (layer, 0, 0, 0)),
            pl.BlockSpec((None, CHUNK, CM_C), lambda i, t: (layer, 0, 0)),
        ],
        out_specs=pl.BlockSpec((None, bm, CM_C), lambda i, t: (i, t, 0)),
        out_shape=jax.ShapeDtypeStruct((b, seq, CM_C), BF16),
        compiler_params=_params("arbitrary", "arbitrary"),
        name="chunk_mlp",
    )(p, p, p, ln_g.reshape(-1, 1, CM_C), ln_b.reshape(-1, 1, CM_C), w_s, bs_full)


def _rms_heads(x, g, nheads):
    outs = []
    for h in range(nheads):
        xh = x[:, h * HEAD_DIM:(h + 1) * HEAD_DIM]
        ms = jnp.mean(xh * xh, axis=-1, keepdims=True)
        outs.append(xh * lax.rsqrt(ms + EPS) * g)
    return outs


def _rope(xh, cos, sin):
    lane = lax.broadcasted_iota(jnp.int32, xh.shape, 1)
    quarter = ROPE_AXIS // 2
    partner = jnp.where((lane % ROPE_AXIS) < quarter,
                        pltpu.roll(xh, HEAD_DIM - quarter, 1), pltpu.roll(xh, quarter, 1))
    return xh * cos + partner * sin


def _kv_prep_body(k_ref, v_ref, gk_ref, cos_ref, sin_ref, kn_ref, kr_ref, vt_ref, *, rotary):
    heads = _rms_heads(k_ref[...], gk_ref[...], N_KV_HEADS)
    for h, kh in enumerate(heads):
        c = slice(h * HEAD_DIM, (h + 1) * HEAD_DIM)
        kn_ref[:, c] = kh
        if rotary:
            kh = _rope(kh, cos_ref[...], sin_ref[...])
        kr_ref[:, c] = kh.astype(BF16)
        vt_ref[h, 0:HEAD_DIM, :] = v_ref[:, c].T.astype(BF16)
        vt_ref[h, HEAD_DIM:, :] = jnp.ones((ATT_ONES_ROWS, v_ref.shape[0]), BF16)


def _kv_prep(p, k_norm_g, cos, sin, layer, rotary):
    b, seq, _ = p.shape
    bm = min(seq, 512)
    blk = pl.BlockSpec((None, bm, KV_C), lambda i, t: (i, t, 0))
    vt_blk = pl.BlockSpec((None, N_KV_HEADS, HEAD_DIM + ATT_ONES_ROWS, bm), lambda i, t: (i, 0, 0, t))
    return pl.pallas_call(
        functools.partial(_kv_prep_body, rotary=rotary),
        grid=(b, seq // bm),
        in_specs=[
            pl.BlockSpec((None, bm, KV_C), lambda i, t: (i, t, OFF_K // KV_C)),
            pl.BlockSpec((None, bm, KV_C), lambda i, t: (i, t, OFF_V // KV_C)),
            pl.BlockSpec((None, 1, HEAD_DIM), lambda i, t: (layer, 0, 0)),
            pl.BlockSpec((bm, HEAD_DIM), lambda i, t: (t, 0)),
            pl.BlockSpec((bm, HEAD_DIM), lambda i, t: (t, 0)),
        ],
        out_specs=[blk, blk, vt_blk],
        out_shape=[jax.ShapeDtypeStruct((b, seq, KV_C), F32),
                   jax.ShapeDtypeStruct((b, seq, KV_C), BF16),
                   jax.ShapeDtypeStruct((b, N_KV_HEADS, HEAD_DIM + ATT_ONES_ROWS, seq), BF16)],
        compiler_params=_params("arbitrary", "arbitrary"),
        name="kv_prep",
    )(p, p, k_norm_g.reshape(-1, 1, HEAD_DIM), cos, sin)


def _attention_body(*refs, rotary, has_ctx):
    q_ref, g_ref, gq_ref, cos_ref, sin_ref, k_ref, vt_ref = refs[:7]
    if has_ctx:
        ck_ref, cv_ref, o_ref, s_ref, p_ref = refs[7:]
    else:
        o_ref, s_ref, p_ref = refs[7:]
    nsub, s_all, nq = s_ref.shape
    tq = nq // GQA
    s_self = k_ref.shape[0]
    chunk = min(ATT_KEY_CHUNK, s_self)
    sub = ROW_GROUP
    scale = math.log2(math.e) / math.sqrt(HEAD_DIM)
    contract_last = (((1,), (1,)), ((), ()))
    key_blocks = [slice(j * chunk, (j + 1) * chunk) for j in range(s_self // chunk)]
    if has_ctx:
        ck = ck_ref[...].astype(BF16)
        cvt = jnp.concatenate([cv_ref[...].T.astype(BF16),
                               jnp.ones((ATT_ONES_ROWS, cv_ref.shape[0]), BF16)], axis=0)

    def col_reduce(x, op):
        return op(x.reshape(x.shape[0] // sub, sub, nq), axis=0)

    blocks = key_blocks + ([slice(s_self, s_all)] if has_ctx else [])

    def scores(t):
        rows_q = slice(t * tq, (t + 1) * tq)
        heads = _rms_heads(q_ref[rows_q, :], gq_ref[...], GQA)
        if rotary:
            heads = [_rope(qh, cos_ref[rows_q, :], sin_ref[rows_q, :]) for qh in heads]
        q = jnp.concatenate([(qh * scale).astype(BF16) for qh in heads], axis=0)
        m8 = None
        for i, rows in enumerate(blocks):
            kb = ck if (has_ctx and i == len(blocks) - 1) else k_ref[rows, :]
            st = lax.dot_general(kb, q, contract_last, preferred_element_type=F32)
            s_ref[t, rows, :] = st
            part = col_reduce(st, jnp.max)
            m8 = part if m8 is None else jnp.maximum(m8, part)
            yield None if i < len(blocks) - 1 else jnp.max(m8, axis=0, keepdims=True)

    def weights(t, m):
        for rows in blocks:
            p_ref[t, rows, :] = jnp.exp2(s_ref[t, rows, :] - m).astype(BF16)
            yield None

    def finish(t):
        rows_q = slice(t * tq, (t + 1) * tq)
        ot = _dot(vt_ref[...], p_ref[t, 0:s_self, :])
        if has_ctx:
            ot = ot + _dot(cvt, p_ref[t, s_self:, :])
        ot = ot[0:HEAD_DIM] / ot[HEAD_DIM:HEAD_DIM + 1]
        for h in range(GQA):
            c = slice(h * HEAD_DIM, (h + 1) * HEAD_DIM)
            oh = ot[:, h * tq:(h + 1) * tq].T
            o_ref[rows_q, c] = (oh * _silu(g_ref[rows_q, c])).astype(o_ref.dtype)

    m_prev = None
    for t in range(nsub + 1):
        score_steps = scores(t) if t < nsub else None
        weight_steps = weights(t - 1, m_prev) if t > 0 else None
        m_cur = None
        for _ in blocks:
            if score_steps is not None:
                m_cur = next(score_steps)
            if weight_steps is not None:
                next(weight_steps)
        if t > 0:
            finish(t - 1)
        m_prev = m_cur


def _attention(p, q_norm_g, cos, sin, keys, values_t, ctx_k, ctx_v, layer, rotary, tq):
    b, seq, _ = p.shape
    skv = keys.shape[1]
    gw = GQA * HEAD_DIM
    has_ctx = ctx_k is not None
    in_specs = [
        pl.BlockSpec((None, tq, gw), lambda i, h, t: (i, t, OFF_Q // gw + h)),
        pl.BlockSpec((None, tq, gw), lambda i, h, t: (i, t, OFF_ATG // gw + h)),
        pl.BlockSpec((None, 1, HEAD_DIM), lambda i, h, t: (layer, 0, 0)),
        pl.BlockSpec((tq, HEAD_DIM), lambda i, h, t: (t, 0)),
        pl.BlockSpec((tq, HEAD_DIM), lambda i, h, t: (t, 0)),
        pl.BlockSpec((None, skv, HEAD_DIM), lambda i, h, t: (i, 0, h)),
        pl.BlockSpec((None, None, HEAD_DIM + ATT_ONES_ROWS, skv), lambda i, h, t: (i, h, 0, 0)),
    ]
    args = [p, p, q_norm_g.reshape(-1, 1, HEAD_DIM), cos, sin, keys, values_t]
    s_all = skv
    if has_ctx:
        past = ctx_k.shape[2]
        spec = pl.BlockSpec((None, None, past, HEAD_DIM), lambda i, h, t: (i, layer, 0, h))
        in_specs += [spec, spec]
        args += [ctx_k, ctx_v]
        s_all += past
    nsub = tq // ATT_SUBTILE_ROWS
    score_cols = GQA * ATT_SUBTILE_ROWS
    return pl.pallas_call(
        functools.partial(_attention_body, rotary=rotary, has_ctx=has_ctx),
        grid=(b, N_KV_HEADS, seq // tq),
        in_specs=in_specs,
        out_specs=pl.BlockSpec((None, tq, gw), lambda i, h, t: (i, t, h)),
        out_shape=jax.ShapeDtypeStruct((b, seq, ATT_C), BF16),
        scratch_shapes=[pltpu.VMEM((nsub, s_all, score_cols), F32), pltpu.VMEM((nsub, s_all, score_cols), BF16)],
        compiler_params=_params("arbitrary", "arbitrary", "arbitrary"),
        name="attention",
    )(*args)


def _out_proj_body(yh_ref, yc_ref, ya_ref, x_ref, mod_ref, w_ref, lg_ref, lb_ref, o_ref):
    d = x_ref.shape[-1]
    out = _dot(yh_ref[...], w_ref[0:HY_C, :])
    out = out + _dot(yc_ref[...], w_ref[HY_C:HY_C + CM_C, :])
    out = out + _dot(ya_ref[...], w_ref[HY_C + CM_C:D_MIX, :])
    gate = mod_ref[:, 2 * d:3 * d]
    r = DN_ALPHA * x_ref[...] + gate * out
    mu = jnp.mean(r, axis=-1, keepdims=True)
    rc = r - mu
    var = jnp.mean(rc * rc, axis=-1, keepdims=True)
    o_ref[...] = rc * lax.rsqrt(var + EPS) * lg_ref[...] + lb_ref[...]


def _out_proj(y_hy, y_cm, y_at, x, mod, row0, w_out_bf, ln_g, ln_b, layer):
    g, t, d = x.shape
    bm = 512

    def rows(c):
        return pl.BlockSpec((None, bm, c), lambda b, i: (b, i, 0))

    return pl.pallas_call(
        _out_proj_body,
        grid=(g, t // bm),
        in_specs=[
            rows(HY_C), rows(CM_C), rows(ATT_C), rows(d),
            pl.BlockSpec((None, None, 1, 3 * d), lambda b, i: (layer, row0 + b, 0, 0)),
            pl.BlockSpec((None, D_MIX, d), lambda b, i: (layer, 0, 0)),
            pl.BlockSpec((None, 1, d), lambda b, i: (layer, 0, 0)),
            pl.BlockSpec((None, 1, d), lambda b, i: (layer, 0, 0)),
        ],
        out_specs=rows(d),
        out_shape=jax.ShapeDtypeStruct((g, t, d), F32),
        compiler_params=_params("arbitrary", "arbitrary"),
        name="out_proj_ln",
    )(y_hy, y_cm, y_at, x, mod, w_out_bf, ln_g.reshape(-1, 1, d), ln_b.reshape(-1, 1, d))


@functools.lru_cache(maxsize=None)
def _rope_tables(seq):
    rows = seq // GRID_W
    row = np.repeat(np.arange(rows, dtype=np.float32), GRID_W)
    col = np.tile(np.arange(GRID_W, dtype=np.float32), rows)
    inv = (np.float32(ROPE_THETA) ** (-np.arange(0, ROPE_AXIS, 2, dtype=np.float32) / np.float32(ROPE_AXIS))
           ).astype(np.float32)
    ra = row[:, None] * inv[None, :]
    ca = col[:, None] * inv[None, :]
    cos = np.concatenate([np.cos(ra), np.cos(ra), np.cos(ca), np.cos(ca)], axis=1)
    sin = np.concatenate([-np.sin(ra), np.sin(ra), -np.sin(ca), np.sin(ca)], axis=1)
    return jnp.asarray(cos, F32), jnp.asarray(sin, F32)


def _mixer_layer(x, seq, mod, row0, layer, rotary, w_in_bf, w_out_bf, spectra, hy_conv_w, hy_conv_b, hy_skip,
                 cm_ln_g, cm_ln_b, cm_w_s, cm_b_s, q_norm_g, k_norm_g, ln_g, ln_b, ctx_k, ctx_v, cos, sin):
    g, t, d = x.shape
    nreq = g * t // seq
    p = _in_proj(x, mod, row0, w_in_bf, layer).reshape(nreq, seq, D_IN)
    if rotary:
        y_hy = _hyena_long(p, hy_conv_w, hy_conv_b, hy_skip, spectra, layer)
    else:
        y_hy = _hyena_short(p, hy_conv_w, hy_conv_b, hy_skip, spectra, layer)
    y_cm = _chunk_mlp(p, cm_ln_g, cm_ln_b, cm_w_s, cm_b_s, layer)
    k_normed, keys, values = _kv_prep(p, k_norm_g, cos, sin, layer, rotary)
    y_at = _attention(p, q_norm_g, cos, sin, keys, values, ctx_k, ctx_v, layer, rotary,
                      tq=min(seq, 2 * ATT_SUBTILE_ROWS))
    x_new = _out_proj(y_hy.reshape(g, t, HY_C), y_cm.reshape(g, t, CM_C), y_at.reshape(g, t, ATT_C),
                      x, mod, row0, w_out_bf, ln_g, ln_b, layer)
    return x_new, k_normed, p


def kernel(x_prompt, x_sample, cache_k, cache_v, c, c_ctx, w_mod, b_mod, w_in, hy_conv_w, hy_conv_b, hy_f_w1, hy_f_b1, hy_f_w2, hy_f_b2, hy_f_w3, hy_f_freq, hy_skip, cm_ln_g, cm_ln_b, cm_w_s, cm_b_s, q_norm_g, k_norm_g, w_out, ln_g, ln_b):
    batch, seq, d = x_prompt.shape
    dec_batch, dec_seq, _ = x_sample.shape
    depth = w_in.shape[0]
    past = cache_k.shape[2]

    cond = jnp.concatenate([c_ctx[None, :], c, jnp.zeros((MOD_ROWS - 1 - dec_batch, d), F32)], axis=0)
    mod = _modulation(cond, w_mod, b_mod).reshape(depth, MOD_ROWS, 1, 3 * d)

    w_in_bf = w_in.astype(BF16)
    w_out_bf = w_out.astype(BF16)
    fweights = _filter_weights(hy_f_w1, hy_f_b1, hy_f_w2, hy_f_b2, hy_f_w3, hy_f_freq)
    spectra_ctx = _filter_spectra_one_stage(seq, fweights)
    spectra_lat = _filter_spectra_two_stage(dec_seq, fweights)
    conv_w = hy_conv_w.reshape(depth, SHORT_K, HY_ORDER + 1, HY_C).transpose(0, 2, 1, 3)
    conv_b = hy_conv_b.reshape(depth, HY_ORDER + 1, HY_C)
    cos, sin = _rope_tables(dec_seq)
    ctx_k = cache_k.reshape(dec_batch, depth, past, KV_C)
    ctx_v = cache_v.reshape(dec_batch, depth, past, KV_C)

    xp = x_prompt.reshape(1, batch * seq, d)
    xs = x_sample
    new_k, new_v = [], []
    for layer in range(depth):
        common = dict(layer=layer, w_in_bf=w_in_bf, w_out_bf=w_out_bf, hy_conv_w=conv_w, hy_conv_b=conv_b,
                      hy_skip=hy_skip, cm_ln_g=cm_ln_g, cm_ln_b=cm_ln_b, cm_w_s=cm_w_s, cm_b_s=cm_b_s,
                      q_norm_g=q_norm_g, k_norm_g=k_norm_g, ln_g=ln_g, ln_b=ln_b, cos=cos, sin=sin)
        xp, k_l, p_ctx = _mixer_layer(xp, seq, mod, 0, rotary=False, spectra=spectra_ctx,
                                      ctx_k=None, ctx_v=None, **common)
        new_k.append(k_l.reshape(batch, seq, N_KV_HEADS, HEAD_DIM))
        new_v.append(p_ctx[:, :, OFF_V:OFF_V + KV_C].reshape(batch, seq, N_KV_HEADS, HEAD_DIM))
        xs, _, _ = _mixer_layer(xs, dec_seq, mod, 1, rotary=True, spectra=spectra_lat,
                                ctx_k=ctx_k, ctx_v=ctx_v, **common)
    return (xp.reshape(batch, seq, d), xs, jnp.stack(new_k, axis=1), jnp.stack(new_v, axis=1))
```

```python
import functools
import math

import numpy as np
import jax
import jax.numpy as jnp
from jax import lax
from jax.experimental import pallas as pl
from jax.experimental.pallas import tpu as pltpu

F32 = jnp.float32
BF16 = jnp.bfloat16

D_MODEL = 2048
DEPTH = 4
GRID_W = 64
HY_C = 512
HY_ORDER = 2
SHORT_K = 3
FILT_EMB = 33
FILT_HID = 64
HY_FAST_DECAY = 0.3
HY_SLOW_DECAY = 1.5
HY_TARGET = 1e-2
CM_HEADS = 4
CM_HD = 128
CM_C = CM_HEADS * CM_HD
CHUNK = 128
N_HEADS = 8
N_KV_HEADS = 2
HEAD_DIM = 128
GQA = N_HEADS // N_KV_HEADS
ATT_C = N_HEADS * HEAD_DIM
KV_C = N_KV_HEADS * HEAD_DIM
ROPE_THETA = 10000.0
ROPE_AXIS = HEAD_DIM // 2
D_MIX = HY_C + CM_C + ATT_C
EPS = 1e-6
DN_ALPHA = (2 * DEPTH) ** 0.25

OFF_HYZ = 0
OFF_HYG = (HY_ORDER + 1) * HY_C
OFF_CMU = OFF_HYG + HY_C
OFF_CMV = OFF_CMU + CM_C
OFF_CMG = OFF_CMV + CM_C
OFF_Q = OFF_CMG + CM_C
OFF_K = OFF_Q + ATT_C
OFF_V = OFF_K + KV_C
OFF_ATG = OFF_V + KV_C
D_IN = OFF_ATG + ATT_C

LANES = 128
VMEM_LIMIT_BYTES = 56 * 1024 * 1024

DFT_INNER = 128
ROW_GROUP = 8
LANE_PAIR = 2
ATT_KEY_CHUNK = 512
ATT_SUBTILE_ROWS = 128
ATT_ONES_ROWS = 16
MOD_ROWS = 16


def _params(*semantics):
    return pltpu.CompilerParams(dimension_semantics=semantics, vmem_limit_bytes=VMEM_LIMIT_BYTES)


def _silu(x):
    return x * jax.nn.sigmoid(x)


def _dot(a, b):
    return jnp.dot(a, b, preferred_element_type=F32)


def _dot_exact(a, b):
    return jnp.dot(a, b, preferred_element_type=F32, precision=lax.Precision.HIGHEST)


def _mod_body(c_ref, w_ref, b_ref, o_ref):
    s = _silu(c_ref[...]).astype(BF16)
    o_ref[...] = _dot(s, w_ref[...].astype(BF16)) + b_ref[...]


def _modulation(cond, w_mod, b_mod):
    depth, d, d3 = w_mod.shape
    bn = 1536
    return pl.pallas_call(
        _mod_body,
        grid=(depth, d3 // bn),
        in_specs=[
            pl.BlockSpec((MOD_ROWS, d), lambda l, j: (0, 0)),
            pl.BlockSpec((None, d, bn), lambda l, j: (l, 0, j)),
            pl.BlockSpec((None, 1, bn), lambda l, j: (l, 0, j)),
        ],
        out_specs=pl.BlockSpec((None, MOD_ROWS, bn), lambda l, j: (l, 0, j)),
        out_shape=jax.ShapeDtypeStruct((depth, MOD_ROWS, d3), F32),
        compiler_params=_params("arbitrary", "arbitrary"),
        name="modulation",
    )(cond, w_mod, b_mod.reshape(depth, 1, d3))


def _in_proj_body(x_ref, mod_ref, w_ref, o_ref):
    d = x_ref.shape[-1]
    shift = mod_ref[:, 0:d]
    scale = mod_ref[:, d:2 * d]
    h = (x_ref[...] * (1.0 + scale) + shift).astype(BF16)
    o_ref[...] = _dot(h, w_ref[...])


def _in_proj(x, mod, row0, w_in_bf, layer):
    g, t, d = x.shape
    d_in = w_in_bf.shape[-1]
    bm, bn = min(t, 2048), 512
    return pl.pallas_call(
        _in_proj_body,
        grid=(g, t // bm, d_in // bn),
        in_specs=[
            pl.BlockSpec((None, bm, d), lambda b, i, j: (b, i, 0)),
            pl.BlockSpec((None, None, 1, 3 * d), lambda b, i, j: (layer, row0 + b, 0, 0)),
            pl.BlockSpec((None, d, bn), lambda b, i, j: (layer, 0, j)),
        ],
        out_specs=pl.BlockSpec((None, bm, bn), lambda b, i, j: (b, i, j)),
        out_shape=jax.ShapeDtypeStruct((g, t, d_in), F32),
        compiler_params=_params("arbitrary", "arbitrary", "arbitrary"),
        name="in_proj",
    )(x, mod, w_in_bf)


def _cs(k, n):
    ang = 2.0 * np.pi * (np.asarray(k, dtype=np.int64) % n).astype(np.float64) / n
    return np.cos(ang), np.sin(ang)


@functools.lru_cache(maxsize=None)
def _two_stage_tables(seq):
    n = 2 * seq
    n2 = DFT_INNER
    n1 = n // n2
    h1 = n1 // 2
    g = ROW_GROUP
    nf = -(-(n1 // 2 + 1) // LANE_PAIR) * LANE_PAIR
    t1 = np.arange(h1)
    f1 = np.arange(nf)
    live = (f1 <= n1 // 2).astype(np.float64)
    mirror_weight = np.where((f1 == 0) | (f1 == n1 // 2), 1.0, 2.0) * live
    t2 = np.arange(n2)
    f2 = np.arange(n2)
    lo = np.arange(g)
    eye = np.eye(g)
    k = t1[None, None, :, None] * f1[:, None, None, None] * n2 + lo[None, None, None, :] * f1[:, None, None, None]
    c, s = _cs(k, n)
    c = c * eye[None, :, None, :] * live[:, None, None, None]
    s = s * eye[None, :, None, :] * live[:, None, None, None]
    ma = np.concatenate([c.reshape(nf * g, h1 * g), -s.reshape(nf * g, h1 * g)], axis=0)
    hi = np.arange(n2 // g)
    c, s = _cs(g * hi[:, None] * f1[None, :], n)
    twc = np.broadcast_to(c[:, :, None], c.shape + (LANES,))
    tws = np.broadcast_to(-s[:, :, None], s.shape + (LANES,))
    c, s = _cs(f2[:, None] * t2[None, :], n2)
    fb = np.block([[c, s], [-s, c]])
    c, s = _cs(t2[:, None] * f2[None, :], n2)
    fbi = np.block([[c, -s], [s, c]])
    t1o = np.arange(h1) + h1 // 2
    k = t1o[:, None, None, None] * f1[None, None, :, None] * n2 + lo[None, None, None, :] * f1[None, None, :, None]
    c, s = _cs(k, n)
    c = c * eye[None, :, None, :] * mirror_weight[None, None, :, None]
    s = s * eye[None, :, None, :] * mirror_weight[None, None, :, None]
    mc = np.concatenate([c.reshape(h1 * g, nf * g), -s.reshape(h1 * g, nf * g)], axis=1)
    return (jnp.asarray(ma, BF16), jnp.asarray(fb, BF16), jnp.asarray(fbi, BF16), jnp.asarray(mc, BF16),
            jnp.asarray(twc, F32), jnp.asarray(tws, F32))


@functools.lru_cache(maxsize=None)
def _one_stage_tables(seq):
    n = 2 * seq
    f = np.arange(n)
    t = np.arange(seq)
    c, s = _cs(f[:, None] * t[None, :], n)
    fwd = np.concatenate([c, -s], axis=0)
    to = np.arange(seq) + seq // 2
    c, s = _cs(to[:, None] * f[None, :], n)
    inv = np.concatenate([c, -s], axis=1)
    return jnp.asarray(fwd, BF16), jnp.asarray(inv, BF16)


@functools.lru_cache(maxsize=None)
def _filter_constants(seq):
    pos = np.arange(seq, dtype=np.float32)
    t = pos / np.float32(max(seq - 1, 1))
    bands = (FILT_EMB - 1) // 2
    fb = np.linspace(1e-4, bands - 1, bands, dtype=np.float32)
    ang = np.float32(2.0 * math.pi / seq) * pos[:, None] * fb[None, :]
    feats = np.zeros((seq, FILT_HID), np.float32)
    feats[:, 0] = t
    feats[:, 1:1 + bands] = np.cos(ang)
    feats[:, 1 + bands:FILT_EMB] = -np.sin(ang)
    dist = np.abs(pos - (seq // 2)) / np.float32(seq / 2.0)
    dist = np.broadcast_to(dist[:, None], (seq, LANES)).astype(np.float32)
    deltas = np.abs(np.linspace(math.log(HY_TARGET) / HY_SLOW_DECAY, math.log(HY_TARGET) / HY_FAST_DECAY,
                                HY_C, dtype=np.float32))
    deltas = np.tile(deltas, HY_ORDER)[None, :]
    return jnp.asarray(feats), jnp.asarray(dist), jnp.asarray(deltas)


def _filter_hidden_body(feats_ref, w1_ref, b1_ref, w2_ref, b2_ref, fr_ref, o_ref):
    fr = fr_ref[...]
    h = jnp.sin(fr * (_dot_exact(feats_ref[...], w1_ref[...]) + b1_ref[...]))
    o_ref[...] = jnp.sin(fr * (_dot_exact(h, w2_ref[...]) + b2_ref[...]))


def _filter_hidden(seq, fweights):
    w1, b1, w2, b2, _, freq = fweights
    depth = w1.shape[0]
    feats, _, _ = _filter_constants(seq)
    mat = pl.BlockSpec((None, FILT_HID, FILT_HID), lambda l: (l, 0, 0))
    vec = pl.BlockSpec((None, 1, FILT_HID), lambda l: (l, 0, 0))
    return pl.pallas_call(
        _filter_hidden_body,
        grid=(depth,),
        in_specs=[pl.BlockSpec((seq, FILT_HID), lambda l: (0, 0)), mat, vec, mat, vec, vec],
        out_specs=pl.BlockSpec((None, seq, FILT_HID), lambda l: (l, 0, 0)),
        out_shape=jax.ShapeDtypeStruct((depth, seq, FILT_HID), F32),
        compiler_params=_params("arbitrary"),
        name="hyena_filter_hidden",
    )(feats, w1, b1, w2, b2, freq)


def _implicit_filter(hid_ref, dist_ref, delt_ref, w3_ref):
    return _dot_exact(hid_ref[...], w3_ref[...]) * jnp.exp(-dist_ref[...] * delt_ref[...])


def _forward_two_stage(src_ref, a_ref, ma_ref, twc_ref, tws_ref, n1, emit):
    n2 = DFT_INNER
    g = ROW_GROUP
    h1 = n1 // 2
    nf = twc_ref.shape[1]
    cw = src_ref.shape[-1]
    half = nf * g

    def stage_a(step, carry):
        his = [step * LANE_PAIR + j for j in range(LANE_PAIR)]
        xs = [src_ref[:, pl.ds(pl.multiple_of(hi * g, g), g), :].reshape(h1 * g, cw).astype(BF16) for hi in his]
        out = _dot(ma_ref[...], jnp.concatenate(xs, axis=1))
        for j, hi in enumerate(his):
            lanes = slice(j * cw, (j + 1) * cw)
            twc = twc_ref[hi]
            tws = tws_ref[hi]
            for f1 in range(nf):
                c = twc[f1:f1 + 1]
                s = tws[f1:f1 + 1]
                re = out[f1 * g:(f1 + 1) * g, lanes]
                im = out[half + f1 * g:half + (f1 + 1) * g, lanes]
                a_ref[hi, f1 * g:(f1 + 1) * g, :] = re * c - im * s
                a_ref[hi, half + f1 * g:half + (f1 + 1) * g, :] = re * s + im * c
        return carry

    lax.fori_loop(0, n2 // g // LANE_PAIR, stage_a, 0, unroll=2)

    for f0 in range(0, nf, LANE_PAIR):
        zs = []
        for f1 in range(f0, f0 + LANE_PAIR):
            ar = a_ref[:, f1 * g:(f1 + 1) * g, :].reshape(n2, cw)
            ai = a_ref[:, half + f1 * g:half + (f1 + 1) * g, :].reshape(n2, cw)
            zs.append(jnp.concatenate([ar, ai], axis=0).astype(BF16))
        emit(f0, jnp.concatenate(zs, axis=1))


def _filter_two_stage_body(hid_ref, dist_ref, delt_ref, w3_ref, ma_ref, fb_ref, twc_ref, tws_ref,
                           o_ref, h_ref, a_ref, *, n1):
    h = _implicit_filter(hid_ref, dist_ref, delt_ref, w3_ref)
    h_ref[...] = h.reshape(h_ref.shape)
    inv_n = 1.0 / (n1 * DFT_INNER)

    cw = h_ref.shape[-1]

    def emit(f1, z):
        spec = (_dot(fb_ref[...], z) * inv_n).astype(o_ref.dtype)
        for j in range(LANE_PAIR):
            o_ref[f1 + j] = spec[:, j * cw:(j + 1) * cw]

    _forward_two_stage(h_ref, a_ref, ma_ref, twc_ref, tws_ref, n1, emit)


def _filter_one_stage_body(hid_ref, dist_ref, delt_ref, w3_ref, fwd_ref, o_ref):
    h = _implicit_filter(hid_ref, dist_ref, delt_ref, w3_ref)
    inv_n = 1.0 / fwd_ref.shape[0] * 2.0
    o_ref[...] = (_dot(fwd_ref[...], h.astype(BF16)) * inv_n).astype(o_ref.dtype)


def _filter_input_specs(seq, cw):
    return [
        pl.BlockSpec((None, seq, FILT_HID), lambda l, j: (l, 0, 0)),
        pl.BlockSpec((seq, LANES), lambda l, j: (0, 0)),
        pl.BlockSpec((1, cw), lambda l, j: (0, j)),
        pl.BlockSpec((None, FILT_HID, cw), lambda l, j: (l, 0, j)),
    ]


def _filter_weights(hy_f_w1, hy_f_b1, hy_f_w2, hy_f_b2, hy_f_w3, hy_f_freq):
    depth = hy_f_w1.shape[0]
    w1 = jnp.pad(hy_f_w1, ((0, 0), (0, FILT_HID - FILT_EMB), (0, 0)))
    return (w1, hy_f_b1.reshape(depth, 1, FILT_HID), hy_f_w2, hy_f_b2.reshape(depth, 1, FILT_HID),
            hy_f_w3, hy_f_freq.reshape(depth, 1, FILT_HID))


def _filter_spectra_two_stage(seq, fweights):
    depth = fweights[0].shape[0]
    cw = LANES
    n2 = DFT_INNER
    n1 = 2 * seq // n2
    _, dist, deltas = _filter_constants(seq)
    ma, fb, _, _, twc, tws = _two_stage_tables(seq)
    nf = twc.shape[1]
    ncol = HY_ORDER * HY_C
    return pl.pallas_call(
        functools.partial(_filter_two_stage_body, n1=n1),
        grid=(depth, ncol // cw),
        in_specs=[
            *_filter_input_specs(seq, cw),
            pl.BlockSpec(ma.shape, lambda l, j: (0, 0)),
            pl.BlockSpec(fb.shape, lambda l, j: (0, 0)),
            pl.BlockSpec(twc.shape, lambda l, j: (0, 0, 0)),
            pl.BlockSpec(tws.shape, lambda l, j: (0, 0, 0)),
        ],
        out_specs=pl.BlockSpec((None, nf, 2 * n2, cw), lambda l, j: (l, 0, 0, j)),
        out_shape=jax.ShapeDtypeStruct((depth, nf, 2 * n2, ncol), BF16),
        scratch_shapes=[pltpu.VMEM((n1 // 2, n2, cw), F32),
                        pltpu.VMEM((n2 // ROW_GROUP, 2 * nf * ROW_GROUP, cw), F32)],
        compiler_params=_params("arbitrary", "arbitrary"),
        name="hyena_filter_long",
    )(_filter_hidden(seq, fweights), dist, deltas, fweights[4], ma, fb, twc, tws)


def _filter_spectra_one_stage(seq, fweights):
    depth = fweights[0].shape[0]
    cw = LANES
    _, dist, deltas = _filter_constants(seq)
    fwd, _ = _one_stage_tables(seq)
    ncol = HY_ORDER * HY_C
    return pl.pallas_call(
        _filter_one_stage_body,
        grid=(depth, ncol // cw),
        in_specs=[*_filter_input_specs(seq, cw), pl.BlockSpec(fwd.shape, lambda l, j: (0, 0))],
        out_specs=pl.BlockSpec((None, 4 * seq, cw), lambda l, j: (l, 0, j)),
        out_shape=jax.ShapeDtypeStruct((depth, 4 * seq, ncol), BF16),
        compiler_params=_params("arbitrary", "arbitrary"),
        name="hyena_filter_short",
    )(_filter_hidden(seq, fweights), dist, deltas, fweights[4], fwd)


def _short_conv(x, w_ref, b_ref, part):
    seq = x.shape[0]
    rows = lax.broadcasted_iota(jnp.int32, x.shape, 0)
    prev = jnp.where(rows == 0, 0.0, pltpu.roll(x, 1, 0))
    nxt = jnp.where(rows == seq - 1, 0.0, pltpu.roll(x, seq - 1, 0))
    w = w_ref[part]
    return prev * w[0:1] + x * w[1:2] + nxt * w[2:3] + b_ref[part:part + 1, :]


def _complex_mul(x, h, half):
    xr, xi = x[:half], x[half:]
    hr, hi = h[:half].astype(F32), h[half:].astype(F32)
    return jnp.concatenate([xr * hr - xi * hi, xr * hi + xi * hr], axis=0)


def _hyena_long_body(v_ref, x1_ref, x2_ref, g_ref, cw_ref, cb_ref, skip_ref, hf0_ref, hf1_ref,
                     ma_ref, fb_ref, fbi_ref, mc_ref, twc_ref, tws_ref, o_ref, y_ref, c_ref, a_ref, w_ref,
                     *, n1):
    n2 = DFT_INNER
    g = ROW_GROUP
    h1 = n1 // 2
    seq, cw = v_ref.shape
    nf = twc_ref.shape[1]
    half = nf * g

    def long_conv(hf_ref):
        def emit(f1, z):
            hf = jnp.concatenate([hf_ref[f1 + j] for j in range(LANE_PAIR)], axis=1)
            spec = _complex_mul(_dot(fb_ref[...], z), hf, n2)
            w = _dot(fbi_ref[...], spec.astype(BF16))
            for j in range(LANE_PAIR):
                r0 = (f1 + j) * g
                r1 = half + (f1 + j) * g
                w_ref[:, r0:r0 + g, :] = w[:n2, j * cw:(j + 1) * cw].reshape(n2 // g, g, cw)
                w_ref[:, r1:r1 + g, :] = w[n2:, j * cw:(j + 1) * cw].reshape(n2 // g, g, cw)

        _forward_two_stage(y_ref, a_ref, ma_ref, twc_ref, tws_ref, n1, emit)

        def stage_c(step, carry):
            his = [step * LANE_PAIR + j for j in range(LANE_PAIR)]
            cols = []
            for hi in his:
                res, ims = [], []
                twc = twc_ref[hi]
                tws = tws_ref[hi]
                for f1 in range(nf):
                    c = twc[f1:f1 + 1]
                    s = tws[f1:f1 + 1]
                    re = w_ref[hi, f1 * g:(f1 + 1) * g, :]
                    im = w_ref[hi, half + f1 * g:half + (f1 + 1) * g, :]
                    res.append(re * c + im * s)
                    ims.append(im * c - re * s)
                cols.append(jnp.concatenate(res + ims, axis=0).astype(BF16))
            y = _dot(mc_ref[...], jnp.concatenate(cols, axis=1))
            for j, hi in enumerate(his):
                c_ref[:, pl.ds(pl.multiple_of(hi * g, g), g), :] = y[:, j * cw:(j + 1) * cw].reshape(h1, g, cw)
            return carry

        lax.fori_loop(0, n2 // g // LANE_PAIR, stage_c, 0, unroll=2)

    def load3(ref):
        return ref[...].reshape(seq, cw)

    y_ref[...] = _short_conv(v_ref[...], cw_ref, cb_ref, 0).reshape(y_ref.shape)
    long_conv(hf0_ref)
    y = _short_conv(x1_ref[...], cw_ref, cb_ref, 1) * (load3(c_ref) + skip_ref[0:1, :] * load3(y_ref))
    y_ref[...] = y.reshape(y_ref.shape)
    long_conv(hf1_ref)
    out = _short_conv(x2_ref[...], cw_ref, cb_ref, 2) * (load3(c_ref) + skip_ref[1:2, :] * load3(y_ref))
    o_ref[...] = (out * _silu(g_ref[...])).astype(o_ref.dtype)


def _const_spec(shape):
    nd = len(shape)
    return pl.BlockSpec(shape, lambda *_: (0,) * nd, pipeline_mode=pl.Buffered(1))


def _hyena_long(p, conv_w, conv_b, skip, spectra, layer):
    b, seq, _ = p.shape
    cw = LANES
    n2 = DFT_INNER
    n1 = 2 * seq // n2
    nblk = HY_C // cw
    ma, fb, fbi, mc, twc, tws = _two_stage_tables(seq)
    nf = twc.shape[1]

    def pcol(off):
        return pl.BlockSpec((None, seq, cw), lambda j, i: (i, 0, off // cw + j))

    def hf_spec(order):
        return pl.BlockSpec((None, nf, 2 * n2, cw), lambda j, i: (layer, 0, 0, order * nblk + j),
                            pipeline_mode=pl.Buffered(1))

    stage_rows = (n2 // ROW_GROUP, 2 * nf * ROW_GROUP, cw)
    return pl.pallas_call(
        functools.partial(_hyena_long_body, n1=n1),
        grid=(nblk, b),
        in_specs=[
            pcol(OFF_HYZ), pcol(OFF_HYZ + HY_C), pcol(OFF_HYZ + 2 * HY_C), pcol(OFF_HYG),
            pl.BlockSpec((None, HY_ORDER + 1, SHORT_K, cw), lambda j, i: (layer, 0, 0, j)),
            pl.BlockSpec((None, HY_ORDER + 1, cw), lambda j, i: (layer, 0, j)),
            pl.BlockSpec((None, HY_ORDER, cw), lambda j, i: (layer, 0, j)),
            hf_spec(0), hf_spec(1),
            _const_spec(ma.shape), _const_spec(fb.shape), _const_spec(fbi.shape), _const_spec(mc.shape),
            _const_spec(twc.shape), _const_spec(tws.shape),
        ],
        out_specs=pl.BlockSpec((None, seq, cw), lambda j, i: (i, 0, j)),
        out_shape=jax.ShapeDtypeStruct((b, seq, HY_C), BF16),
        scratch_shapes=[
            pltpu.VMEM((n1 // 2, n2, cw), F32), pltpu.VMEM((n1 // 2, n2, cw), F32),
            pltpu.VMEM(stage_rows, F32), pltpu.VMEM(stage_rows, F32),
        ],
        compiler_params=_params("arbitrary", "arbitrary"),
        name="hyena_long",
    )(p, p, p, p, conv_w, conv_b, skip, spectra, spectra, ma, fb, fbi, mc, twc, tws)


def _hyena_short_body(v_ref, x1_ref, x2_ref, g_ref, cw_ref, cb_ref, skip_ref, hf0_ref, hf1_ref,
                      fwd_ref, inv_ref, o_ref):
    n = fwd_ref.shape[0] // 2

    def long_conv(y, hf_ref):
        spec = _complex_mul(_dot(fwd_ref[...], y.astype(BF16)), hf_ref[...], n)
        return _dot(inv_ref[...], spec.astype(BF16))

    y = _short_conv(v_ref[...], cw_ref, cb_ref, 0)
    y = _short_conv(x1_ref[...], cw_ref, cb_ref, 1) * (long_conv(y, hf0_ref) + skip_ref[0:1, :] * y)
    y = _short_conv(x2_ref[...], cw_ref, cb_ref, 2) * (long_conv(y, hf1_ref) + skip_ref[1:2, :] * y)
    o_ref[...] = (y * _silu(g_ref[...])).astype(o_ref.dtype)


def _hyena_short(p, conv_w, conv_b, skip, spectra, layer):
    b, seq, _ = p.shape
    cw = LANE_PAIR * LANES
    nblk = HY_C // cw
    fwd, inv = _one_stage_tables(seq)

    def pcol(off):
        return pl.BlockSpec((None, seq, cw), lambda j, i: (i, 0, off // cw + j))

    def hf_spec(order):
        return pl.BlockSpec((None, 4 * seq, cw), lambda j, i: (layer, 0, order * nblk + j))

    return pl.pallas_call(
        _hyena_short_body,
        grid=(nblk, b),
        in_specs=[
            pcol(OFF_HYZ), pcol(OFF_HYZ + HY_C), pcol(OFF_HYZ + 2 * HY_C), pcol(OFF_HYG),
            pl.BlockSpec((None, HY_ORDER + 1, SHORT_K, cw), lambda j, i: (layer, 0, 0, j)),
            pl.BlockSpec((None, HY_ORDER + 1, cw), lambda j, i: (layer, 0, j)),
            pl.BlockSpec((None, HY_ORDER, cw), lambda j, i: (layer, 0, j)),
            hf_spec(0), hf_spec(1),
            pl.BlockSpec(fwd.shape, lambda j, i: (0, 0)), pl.BlockSpec(inv.shape, lambda j, i: (0, 0)),
        ],
        out_specs=pl.BlockSpec((None, seq, cw), lambda j, i: (i, 0, j)),
        out_shape=jax.ShapeDtypeStruct((b, seq, HY_C), BF16),
        compiler_params=_params("arbitrary", "arbitrary"),
        name="hyena_short",
    )(p, p, p, p, conv_w, conv_b, skip, spectra, spectra, fwd, inv)


def _chunk_mlp_body(u_ref, v_ref, g_ref, lg_ref, lb_ref, ws_ref, bs_ref, o_ref):
    v = v_ref[...]
    mu = jnp.mean(v, axis=-1, keepdims=True)
    vc = v - mu
    var = jnp.mean(vc * vc, axis=-1, keepdims=True)
    vn = (vc * lax.rsqrt(var + EPS) * lg_ref[...] + lb_ref[...]).astype(BF16)
    rows = v.shape[0]
    for n in range(rows // CHUNK):
        r = slice(n * CHUNK, (n + 1) * CHUNK)
        for h in range(CM_HEADS):
            c = slice(h * CM_HD, (h + 1) * CM_HD)
            s = _dot(ws_ref[h].astype(BF16), vn[r, c]) + bs_ref[:, c]
            o_ref[r, c] = (u_ref[r, c] * s * _silu(g_ref[r, c])).astype(o_ref.dtype)


def _chunk_mlp(p, ln_g, ln_b, w_s, b_s, layer):
    b, seq, _ = p.shape
    bm = min(seq, 512)
    bs_full = jnp.repeat(jnp.swapaxes(b_s, 1, 2), CM_HD, axis=2)

    def pcol(off):
        return pl.BlockSpec((None, bm, CM_C), lambda i, t: (i, t, off // CM_C))

    return pl.pallas_call(
        _chunk_mlp_body,
        grid=(b, seq // bm),
        in_specs=[
            pcol(OFF_CMU), pcol(OFF_CMV), pcol(OFF_CMG),
            pl.BlockSpec((None, 1, CM_C), lambda i, t: (layer, 0, 0)),
            pl.BlockSpec((None, 1, CM_C), lambda i, t: (layer, 0, 0)),
            pl.BlockSpec((None, CM_HEADS, CHUNK, CHUNK), lambda i, t: (layer, 0, 0, 0)),
            pl.BlockSpec((None, CHUNK, CM_C), lambda i, t: (layer, 0, 0)),
        ],
        out_specs=pl.BlockSpec((None, bm, CM_C), lambda i, t: (i, t, 0)),
        out_shape=jax.ShapeDtypeStruct((b, seq, CM_C), BF16),
        compiler_params=_params("arbitrary", "arbitrary"),
        name="chunk_mlp",
    )(p, p, p, ln_g.reshape(-1, 1, CM_C), ln_b.reshape(-1, 1, CM_C), w_s, bs_full)


def _rms_heads(x, g, nheads):
    outs = []
    for h in range(nheads):
        xh = x[:, h * HEAD_DIM:(h + 1) * HEAD_DIM]
        ms = jnp.mean(xh * xh, axis=-1, keepdims=True)
        outs.append(xh * lax.rsqrt(ms + EPS) * g)
    return outs


def _rope(xh, cos, sin):
    lane = lax.broadcasted_iota(jnp.int32, xh.shape, 1)
    quarter = ROPE_AXIS // 2
    partner = jnp.where((lane % ROPE_AXIS) < quarter,
                        pltpu.roll(xh, HEAD_DIM - quarter, 1), pltpu.roll(xh, quarter, 1))
    return xh * cos + partner * sin


def _kv_prep_body(k_ref, v_ref, gk_ref, cos_ref, sin_ref, kn_ref, kr_ref, vt_ref, *, rotary):
    heads = _rms_heads(k_ref[...], gk_ref[...], N_KV_HEADS)
    for h, kh in enumerate(heads):
        c = slice(h * HEAD_DIM, (h + 1) * HEAD_DIM)
        kn_ref[:, c] = kh
        if rotary:
            kh = _rope(kh, cos_ref[...], sin_ref[...])
        kr_ref[:, c] = kh.astype(BF16)
        vt_ref[h, 0:HEAD_DIM, :] = v_ref[:, c].T.astype(BF16)
        vt_ref[h, HEAD_DIM:, :] = jnp.ones((ATT_ONES_ROWS, v_ref.shape[0]), BF16)


def _kv_prep(p, k_norm_g, cos, sin, layer, rotary):
    b, seq, _ = p.shape
    bm = min(seq, 512)
    blk = pl.BlockSpec((None, bm, KV_C), lambda i, t: (i, t, 0))
    vt_blk = pl.BlockSpec((None, N_KV_HEADS, HEAD_DIM + ATT_ONES_ROWS, bm), lambda i, t: (i, 0, 0, t))
    return pl.pallas_call(
        functools.partial(_kv_prep_body, rotary=rotary),
        grid=(b, seq // bm),
        in_specs=[
            pl.BlockSpec((None, bm, KV_C), lambda i, t: (i, t, OFF_K // KV_C)),
            pl.BlockSpec((None, bm, KV_C), lambda i, t: (i, t, OFF_V // KV_C)),
            pl.BlockSpec((None, 1, HEAD_DIM), lambda i, t: (layer, 0, 0)),
            pl.BlockSpec((bm, HEAD_DIM), lambda i, t: (t, 0)),
            pl.BlockSpec((bm, HEAD_DIM), lambda i, t: (t, 0)),
        ],
        out_specs=[blk, blk, vt_blk],
        out_shape=[jax.ShapeDtypeStruct((b, seq, KV_C), F32),
                   jax.ShapeDtypeStruct((b, seq, KV_C), BF16),
                   jax.ShapeDtypeStruct((b, N_KV_HEADS, HEAD_DIM + ATT_ONES_ROWS, seq), BF16)],
        compiler_params=_params("arbitrary", "arbitrary"),
        name="kv_prep",
    )(p, p, k_norm_g.reshape(-1, 1, HEAD_DIM), cos, sin)


def _attention_body(*refs, rotary, has_ctx):
    q_ref, g_ref, gq_ref, cos_ref, sin_ref, k_ref, vt_ref = refs[:7]
    if has_ctx:
        ck_ref, cv_ref, o_ref, s_ref, p_ref = refs[7:]
    else:
        o_ref, s_ref, p_ref = refs[7:]
    nsub, s_all, nq = s_ref.shape
    tq = nq // GQA
    s_self = k_ref.shape[0]
    chunk = min(ATT_KEY_CHUNK, s_self)
    sub = ROW_GROUP
    scale = math.log2(math.e) / math.sqrt(HEAD_DIM)
    contract_last = (((1,), (1,)), ((), ()))
    key_blocks = [slice(j * chunk, (j + 1) * chunk) for j in range(s_self // chunk)]
    if has_ctx:
        ck = ck_ref[...].astype(BF16)
        cvt = jnp.concatenate([cv_ref[...].T.astype(BF16),
                               jnp.ones((ATT_ONES_ROWS, cv_ref.shape[0]), BF16)], axis=0)

    def col_reduce(x, op):
        return op(x.reshape(x.shape[0] // sub, sub, nq), axis=0)

    blocks = key_blocks + ([slice(s_self, s_all)] if has_ctx else [])

    def scores(t):
        rows_q = slice(t * tq, (t + 1) * tq)
        heads = _rms_heads(q_ref[rows_q, :], gq_ref[...], GQA)
        if rotary:
            heads = [_rope(qh, cos_ref[rows_q, :], sin_ref[rows_q, :]) for qh in heads]
        q = jnp.concatenate([(qh * scale).astype(BF16) for qh in heads], axis=0)
        m8 = None
        for i, rows in enumerate(blocks):
            kb = ck if (has_ctx and i == len(blocks) - 1) else k_ref[rows, :]
            st = lax.dot_general(kb, q, contract_last, preferred_element_type=F32)
            s_ref[t, rows, :] = st
            part = col_reduce(st, jnp.max)
            m8 = part if m8 is None else jnp.maximum(m8, part)
            yield None if i < len(blocks) - 1 else jnp.max(m8, axis=0, keepdims=True)

    def weights(t, m):
        for rows in blocks:
            p_ref[t, rows, :] = jnp.exp2(s_ref[t, rows, :] - m).astype(BF16)
            yield None

    def finish(t):
        rows_q = slice(t * tq, (t + 1) * tq)
        ot = _dot(vt_ref[...], p_ref[t, 0:s_self, :])
        if has_ctx:
            ot = ot + _dot(cvt, p_ref[t, s_self:, :])
        ot = ot[0:HEAD_DIM] / ot[HEAD_DIM:HEAD_DIM + 1]
        for h in range(GQA):
            c = slice(h * HEAD_DIM, (h + 1) * HEAD_DIM)
            oh = ot[:, h * tq:(h + 1) * tq].T
            o_ref[rows_q, c] = (oh * _silu(g_ref[rows_q, c])).astype(o_ref.dtype)

    m_prev = None
    for t in range(nsub + 1):
        score_steps = scores(t) if t < nsub else None
        weight_steps = weights(t - 1, m_prev) if t > 0 else None
        m_cur = None
        for _ in blocks:
            if score_steps is not None:
                m_cur = next(score_steps)
            if weight_steps is not None:
                next(weight_steps)
        if t > 0:
            finish(t - 1)
        m_prev = m_cur


def _attention(p, q_norm_g, cos, sin, keys, values_t, ctx_k, ctx_v, layer, rotary, tq):
    b, seq, _ = p.shape
    skv = keys.shape[1]
    gw = GQA * HEAD_DIM
    has_ctx = ctx_k is not None
    in_specs = [
        pl.BlockSpec((None, tq, gw), lambda i, h, t: (i, t, OFF_Q // gw + h)),
        pl.BlockSpec((None, tq, gw), lambda i, h, t: (i, t, OFF_ATG // gw + h)),
        pl.BlockSpec((None, 1, HEAD_DIM), lambda i, h, t: (layer, 0, 0)),
        pl.BlockSpec((tq, HEAD_DIM), lambda i, h, t: (t, 0)),
        pl.BlockSpec((tq, HEAD_DIM), lambda i, h, t: (t, 0)),
        pl.BlockSpec((None, skv, HEAD_DIM), lambda i, h, t: (i, 0, h)),
        pl.BlockSpec((None, None, HEAD_DIM + ATT_ONES_ROWS, skv), lambda i, h, t: (i, h, 0, 0)),
    ]
    args = [p, p, q_norm_g.reshape(-1, 1, HEAD_DIM), cos, sin, keys, values_t]
    s_all = skv
    if has_ctx:
        past = ctx_k.shape[2]
        spec = pl.BlockSpec((None, None, past, HEAD_DIM), lambda i, h, t: (i, layer, 0, h))
        in_specs += [spec, spec]
        args += [ctx_k, ctx_v]
        s_all += past
    nsub = tq // ATT_SUBTILE_ROWS
    score_cols = GQA * ATT_SUBTILE_ROWS
    return pl.pallas_call(
        functools.partial(_attention_body, rotary=rotary, has_ctx=has_ctx),
        grid=(b, N_KV_HEADS, seq // tq),
        in_specs=in_specs,
        out_specs=pl.BlockSpec((None, tq, gw), lambda i, h, t: (i, t, h)),
        out_shape=jax.ShapeDtypeStruct((b, seq, ATT_C), BF16),
        scratch_shapes=[pltpu.VMEM((nsub, s_all, score_cols), F32), pltpu.VMEM((nsub, s_all, score_cols), BF16)],
        compiler_params=_params("arbitrary", "arbitrary", "arbitrary"),
        name="attention",
    )(*args)


def _out_proj_body(yh_ref, yc_ref, ya_ref, x_ref, mod_ref, w_ref, lg_ref, lb_ref, o_ref):
    d = x_ref.shape[-1]
    gate = mod_ref[:, 2 * d:3 * d]
    mixed = jnp.concatenate([yh_ref[...], yc_ref[...], ya_ref[...]], axis=1)
    r = DN_ALPHA * x_ref[...] + gate * _dot(mixed, w_ref[...])
    mu = jnp.mean(r, axis=-1, keepdims=True)
    rc = r - mu
    var = jnp.mean(rc * rc, axis=-1, keepdims=True)
    o_ref[...] = rc * lax.rsqrt(var + EPS) * lg_ref[...] + lb_ref[...]


def _out_proj(y_hy, y_cm, y_at, x, mod, row0, w_out_bf, ln_g, ln_b, layer):
    g, t, d = x.shape
    bm = 512

    def rows(c):
        return pl.BlockSpec((None, bm, c), lambda b, i: (b, i, 0))

    return pl.pallas_call(
        _out_proj_body,
        grid=(g, t // bm),
        in_specs=[
            rows(HY_C), rows(CM_C), rows(ATT_C), rows(d),
            pl.BlockSpec((None, None, 1, 3 * d), lambda b, i: (layer, row0 + b, 0, 0)),
            pl.BlockSpec((None, D_MIX, d), lambda b, i: (layer, 0, 0)),
            pl.BlockSpec((None, 1, d), lambda b, i: (layer, 0, 0)),
            pl.BlockSpec((None, 1, d), lambda b, i: (layer, 0, 0)),
        ],
        out_specs=rows(d),
        out_shape=jax.ShapeDtypeStruct((g, t, d), F32),
        compiler_params=_params("arbitrary", "arbitrary"),
        name="out_proj_ln",
    )(y_hy, y_cm, y_at, x, mod, w_out_bf, ln_g.reshape(-1, 1, d), ln_b.reshape(-1, 1, d))


@functools.lru_cache(maxsize=None)
def _rope_tables(seq):
    rows = seq // GRID_W
    row = np.repeat(np.arange(rows, dtype=np.float32), GRID_W)
    col = np.tile(np.arange(GRID_W, dtype=np.float32), rows)
    inv = (np.float32(ROPE_THETA) ** (-np.arange(0, ROPE_AXIS, 2, dtype=np.float32) / np.float32(ROPE_AXIS))
           ).astype(np.float32)
    ra = row[:, None] * inv[None, :]
    ca = col[:, None] * inv[None, :]
    cos = np.concatenate([np.cos(ra), np.cos(ra), np.cos(ca), np.cos(ca)], axis=1)
    sin = np.concatenate([-np.sin(ra), np.sin(ra), -np.sin(ca), np.sin(ca)], axis=1)
    return jnp.asarray(cos, F32), jnp.asarray(sin, F32)


def _mixer_layer(x, seq, mod, row0, layer, rotary, w_in_bf, w_out_bf, spectra, hy_conv_w, hy_conv_b, hy_skip,
                 cm_ln_g, cm_ln_b, cm_w_s, cm_b_s, q_norm_g, k_norm_g, ln_g, ln_b, ctx_k, ctx_v, cos, sin):
    g, t, d = x.shape
    nreq = g * t // seq
    p = _in_proj(x, mod, row0, w_in_bf, layer).reshape(nreq, seq, D_IN)
    if rotary:
        y_hy = _hyena_long(p, hy_conv_w, hy_conv_b, hy_skip, spectra, layer)
    else:
        y_hy = _hyena_short(p, hy_conv_w, hy_conv_b, hy_skip, spectra, layer)
    y_cm = _chunk_mlp(p, cm_ln_g, cm_ln_b, cm_w_s, cm_b_s, layer)
    k_normed, keys, values = _kv_prep(p, k_norm_g, cos, sin, layer, rotary)
    y_at = _attention(p, q_norm_g, cos, sin, keys, values, ctx_k, ctx_v, layer, rotary,
                      tq=min(seq, 2 * ATT_SUBTILE_ROWS))
    x_new = _out_proj(y_hy.reshape(g, t, HY_C), y_cm.reshape(g, t, CM_C), y_at.reshape(g, t, ATT_C),
                      x, mod, row0, w_out_bf, ln_g, ln_b, layer)
    return x_new, k_normed, p


def kernel(x_prompt, x_sample, cache_k, cache_v, c, c_ctx, w_mod, b_mod, w_in, hy_conv_w, hy_conv_b, hy_f_w1, hy_f_b1, hy_f_w2, hy_f_b2, hy_f_w3, hy_f_freq, hy_skip, cm_ln_g, cm_ln_b, cm_w_s, cm_b_s, q_norm_g, k_norm_g, w_out, ln_g, ln_b):
    batch, seq, d = x_prompt.shape
    dec_batch, dec_seq, _ = x_sample.shape
    depth = w_in.shape[0]
    past = cache_k.shape[2]

    cond = jnp.concatenate([c_ctx[None, :], c, jnp.zeros((MOD_ROWS - 1 - dec_batch, d), F32)], axis=0)
    mod = _modulation(cond, w_mod, b_mod).reshape(depth, MOD_ROWS, 1, 3 * d)

    w_in_bf = w_in.astype(BF16)
    w_out_bf = w_out.astype(BF16)
    fweights = _filter_weights(hy_f_w1, hy_f_b1, hy_f_w2, hy_f_b2, hy_f_w3, hy_f_freq)
    spectra_ctx = _filter_spectra_one_stage(seq, fweights)
    spectra_lat = _filter_spectra_two_stage(dec_seq, fweights)
    conv_w = hy_conv_w.reshape(depth, SHORT_K, HY_ORDER + 1, HY_C).transpose(0, 2, 1, 3)
    conv_b = hy_conv_b.reshape(depth, HY_ORDER + 1, HY_C)
    cos, sin = _rope_tables(dec_seq)
    ctx_k = cache_k.reshape(dec_batch, depth, past, KV_C)
    ctx_v = cache_v.reshape(dec_batch, depth, past, KV_C)

    xp = x_prompt.reshape(1, batch * seq, d)
    xs = x_sample
    new_k, new_v = [], []
    for layer in range(depth):
        common = dict(layer=layer, w_in_bf=w_in_bf, w_out_bf=w_out_bf, hy_conv_w=conv_w, hy_conv_b=conv_b,
                      hy_skip=hy_skip, cm_ln_g=cm_ln_g, cm_ln_b=cm_ln_b, cm_w_s=cm_w_s, cm_b_s=cm_b_s,
                      q_norm_g=q_norm_g, k_norm_g=k_norm_g, ln_g=ln_g, ln_b=ln_b, cos=cos, sin=sin)
        xp, k_l, p_ctx = _mixer_layer(xp, seq, mod, 0, rotary=False, spectra=spectra_ctx,
                                      ctx_k=None, ctx_v=None, **common)
        new_k.append(k_l.reshape(batch, seq, N_KV_HEADS, HEAD_DIM))
        new_v.append(p_ctx[:, :, OFF_V:OFF_V + KV_C].reshape(batch, seq, N_KV_HEADS, HEAD_DIM))
        xs, _, _ = _mixer_layer(xs, dec_seq, mod, 1, rotary=True, spectra=spectra_lat,
                                ctx_k=ctx_k, ctx_v=ctx_v, **common)
    return (xp.reshape(batch, seq, d), xs, jnp.stack(new_k, axis=1), jnp.stack(new_v, axis=1))
```

```python
import functools
import math

import numpy as np
import jax
import jax.numpy as jnp
from jax import lax
from jax.experimental import pallas as pl
from jax.experimental.pallas import tpu as pltpu

F32 = jnp.float32
BF16 = jnp.bfloat16

D_MODEL = 2048
DEPTH = 4
GRID_W = 64
HY_C = 512
HY_ORDER = 2
SHORT_K = 3
FILT_EMB = 33
FILT_HID = 64
HY_FAST_DECAY = 0.3
HY_SLOW_DECAY = 1.5
HY_TARGET = 1e-2
CM_HEADS = 4
CM_HD = 128
CM_C = CM_HEADS * CM_HD
CHUNK = 128
N_HEADS = 8
N_KV_HEADS = 2
HEAD_DIM = 128
GQA = N_HEADS // N_KV_HEADS
ATT_C = N_HEADS * HEAD_DIM
KV_C = N_KV_HEADS * HEAD_DIM
ROPE_THETA = 10000.0
ROPE_AXIS = HEAD_DIM // 2
D_MIX = HY_C + CM_C + ATT_C
EPS = 1e-6
DN_ALPHA = (2 * DEPTH) ** 0.25

OFF_HYZ = 0
OFF_HYG = (HY_ORDER + 1) * HY_C
OFF_CMU = OFF_HYG + HY_C
OFF_CMV = OFF_CMU + CM_C
OFF_CMG = OFF_CMV + CM_C
OFF_Q = OFF_CMG + CM_C
OFF_K = OFF_Q + ATT_C
OFF_V = OFF_K + KV_C
OFF_ATG = OFF_V + KV_C
D_IN = OFF_ATG + ATT_C

LANES = 128
VMEM_LIMIT_BYTES = 56 * 1024 * 1024

DFT_INNER = 128
ROW_GROUP = 8
LANE_PAIR = 2
ATT_KEY_CHUNK = 512
ATT_SUBTILE_ROWS = 128
ATT_ONES_ROWS = 16
ATT_PIPE_SLOTS = 2
ATT_SUBTILES = 8
MOD_ROWS = 16


def _params(*semantics):
    return pltpu.CompilerParams(dimension_semantics=semantics, vmem_limit_bytes=VMEM_LIMIT_BYTES)


def _silu(x):
    return x * jax.nn.sigmoid(x)


def _dot(a, b):
    return jnp.dot(a, b, preferred_element_type=F32)


def _dot_exact(a, b):
    return jnp.dot(a, b, preferred_element_type=F32, precision=lax.Precision.HIGHEST)


def _mod_body(c_ref, w_ref, b_ref, o_ref):
    s = _silu(c_ref[...]).astype(BF16)
    o_ref[...] = _dot(s, w_ref[...].astype(BF16)) + b_ref[...]


def _modulation(cond, w_mod, b_mod):
    depth, d, d3 = w_mod.shape
    bn = 1536
    return pl.pallas_call(
        _mod_body,
        grid=(depth, d3 // bn),
        in_specs=[
            pl.BlockSpec((MOD_ROWS, d), lambda l, j: (0, 0)),
            pl.BlockSpec((None, d, bn), lambda l, j: (l, 0, j)),
            pl.BlockSpec((None, 1, bn), lambda l, j: (l, 0, j)),
        ],
        out_specs=pl.BlockSpec((None, MOD_ROWS, bn), lambda l, j: (l, 0, j)),
        out_shape=jax.ShapeDtypeStruct((depth, MOD_ROWS, d3), F32),
        compiler_params=_params("arbitrary", "arbitrary"),
        name="modulation",
    )(cond, w_mod, b_mod.reshape(depth, 1, d3))


def _in_proj_body(x_ref, mod_ref, w_ref, o_ref):
    d = x_ref.shape[-1]
    shift = mod_ref[:, 0:d]
    scale = mod_ref[:, d:2 * d]
    h = (x_ref[...] * (1.0 + scale) + shift).astype(BF16)
    o_ref[...] = _dot(h, w_ref[...])


def _in_proj(x, mod, row0, w_in_bf, layer):
    g, t, d = x.shape
    d_in = w_in_bf.shape[-1]
    bm, bn = 1024, 1024
    return pl.pallas_call(
        _in_proj_body,
        grid=(g, t // bm, d_in // bn),
        in_specs=[
            pl.BlockSpec((None, bm, d), lambda b, i, j: (b, i, 0)),
            pl.BlockSpec((None, None, 1, 3 * d), lambda b, i, j: (layer, row0 + b, 0, 0)),
            pl.BlockSpec((None, d, bn), lambda b, i, j: (layer, 0, j)),
        ],
        out_specs=pl.BlockSpec((None, bm, bn), lambda b, i, j: (b, i, j)),
        out_shape=jax.ShapeDtypeStruct((g, t, d_in), F32),
        compiler_params=_params("arbitrary", "arbitrary", "arbitrary"),
        name="in_proj",
    )(x, mod, w_in_bf)


def _cs(k, n):
    ang = 2.0 * np.pi * (np.asarray(k, dtype=np.int64) % n).astype(np.float64) / n
    return np.cos(ang), np.sin(ang)


@functools.lru_cache(maxsize=None)
def _two_stage_tables(seq):
    n = 2 * seq
    n2 = DFT_INNER
    n1 = n // n2
    h1 = n1 // 2
    g = ROW_GROUP
    nf = -(-(n1 // 2 + 1) // LANE_PAIR) * LANE_PAIR
    t1 = np.arange(h1)
    f1 = np.arange(nf)
    live = (f1 <= n1 // 2).astype(np.float64)
    mirror_weight = np.where((f1 == 0) | (f1 == n1 // 2), 1.0, 2.0) * live
    t2 = np.arange(n2)
    f2 = np.arange(n2)
    lo = np.arange(g)
    eye = np.eye(g)
    k = t1[None, None, :, None] * f1[:, None, None, None] * n2 + lo[None, None, None, :] * f1[:, None, None, None]
    c, s = _cs(k, n)
    c = c * eye[None, :, None, :] * live[:, None, None, None]
    s = s * eye[None, :, None, :] * live[:, None, None, None]
    ma = np.concatenate([c.reshape(nf * g, h1 * g), -s.reshape(nf * g, h1 * g)], axis=0)
    hi = np.arange(n2 // g)
    c, s = _cs(g * hi[:, None] * f1[None, :], n)
    twc = np.broadcast_to(c[:, :, None], c.shape + (LANES,))
    tws = np.broadcast_to(-s[:, :, None], s.shape + (LANES,))
    c, s = _cs(f2[:, None] * t2[None, :], n2)
    fb = np.block([[c, s], [-s, c]])
    c, s = _cs(t2[:, None] * f2[None, :], n2)
    fbi = np.block([[c, -s], [s, c]])
    t1o = np.arange(h1) + h1 // 2
    k = t1o[:, None, None, None] * f1[None, None, :, None] * n2 + lo[None, None, None, :] * f1[None, None, :, None]
    c, s = _cs(k, n)
    c = c * eye[None, :, None, :] * mirror_weight[None, None, :, None]
    s = s * eye[None, :, None, :] * mirror_weight[None, None, :, None]
    mc = np.concatenate([c.reshape(h1 * g, nf * g), -s.reshape(h1 * g, nf * g)], axis=1)
    return (jnp.asarray(ma, BF16), jnp.asarray(fb, BF16), jnp.asarray(fbi, BF16), jnp.asarray(mc, BF16),
            jnp.asarray(twc, F32), jnp.asarray(tws, F32))


@functools.lru_cache(maxsize=None)
def _one_stage_tables(seq):
    n = 2 * seq
    f = np.arange(n)
    t = np.arange(seq)
    c, s = _cs(f[:, None] * t[None, :], n)
    fwd = np.concatenate([c, -s], axis=0)
    to = np.arange(seq) + seq // 2
    c, s = _cs(to[:, None] * f[None, :], n)
    inv = np.concatenate([c, -s], axis=1)
    return jnp.asarray(fwd, BF16), jnp.asarray(inv, BF16)


@functools.lru_cache(maxsize=None)
def _filter_constants(seq):
    pos = np.arange(seq, dtype=np.float32)
    t = pos / np.float32(max(seq - 1, 1))
    bands = (FILT_EMB - 1) // 2
    fb = np.linspace(1e-4, bands - 1, bands, dtype=np.float32)
    ang = np.float32(2.0 * math.pi / seq) * pos[:, None] * fb[None, :]
    feats = np.zeros((seq, FILT_HID), np.float32)
    feats[:, 0] = t
    feats[:, 1:1 + bands] = np.cos(ang)
    feats[:, 1 + bands:FILT_EMB] = -np.sin(ang)
    dist = np.abs(pos - (seq // 2)) / np.float32(seq / 2.0)
    dist = np.broadcast_to(dist[:, None], (seq, LANES)).astype(np.float32)
    deltas = np.abs(np.linspace(math.log(HY_TARGET) / HY_SLOW_DECAY, math.log(HY_TARGET) / HY_FAST_DECAY,
                                HY_C, dtype=np.float32))
    deltas = np.tile(deltas, HY_ORDER)[None, :]
    return jnp.asarray(feats), jnp.asarray(dist), jnp.asarray(deltas)


def _filter_hidden_body(feats_ref, w1_ref, b1_ref, w2_ref, b2_ref, fr_ref, o_ref):
    fr = fr_ref[...]
    h = jnp.sin(fr * (_dot_exact(feats_ref[...], w1_ref[...]) + b1_ref[...]))
    o_ref[...] = jnp.sin(fr * (_dot_exact(h, w2_ref[...]) + b2_ref[...]))


def _filter_hidden(seq, fweights):
    w1, b1, w2, b2, _, freq = fweights
    depth = w1.shape[0]
    feats, _, _ = _filter_constants(seq)
    mat = pl.BlockSpec((None, FILT_HID, FILT_HID), lambda l: (l, 0, 0))
    vec = pl.BlockSpec((None, 1, FILT_HID), lambda l: (l, 0, 0))
    return pl.pallas_call(
        _filter_hidden_body,
        grid=(depth,),
        in_specs=[pl.BlockSpec((seq, FILT_HID), lambda l: (0, 0)), mat, vec, mat, vec, vec],
        out_specs=pl.BlockSpec((None, seq, FILT_HID), lambda l: (l, 0, 0)),
        out_shape=jax.ShapeDtypeStruct((depth, seq, FILT_HID), F32),
        compiler_params=_params("arbitrary"),
        name="hyena_filter_hidden",
    )(feats, w1, b1, w2, b2, freq)


def _implicit_filter(hid_ref, dist_ref, delt_ref, w3_ref):
    return _dot_exact(hid_ref[...], w3_ref[...]) * jnp.exp(-dist_ref[...] * delt_ref[...])


def _forward_two_stage(src_ref, a_ref, ma_ref, twc_ref, tws_ref, n1, emit):
    n2 = DFT_INNER
    g = ROW_GROUP
    h1 = n1 // 2
    nf = twc_ref.shape[1]
    cw = src_ref.shape[-1]
    half = nf * g

    def stage_a(step, carry):
        his = [step * LANE_PAIR + j for j in range(LANE_PAIR)]
        xs = [src_ref[:, pl.ds(pl.multiple_of(hi * g, g), g), :].reshape(h1 * g, cw).astype(BF16) for hi in his]
        out = _dot(ma_ref[...], jnp.concatenate(xs, axis=1))
        for j, hi in enumerate(his):
            lanes = slice(j * cw, (j + 1) * cw)
            twc = twc_ref[hi]
            tws = tws_ref[hi]
            for f1 in range(nf):
                c = twc[f1:f1 + 1]
                s = tws[f1:f1 + 1]
                re = out[f1 * g:(f1 + 1) * g, lanes]
                im = out[half + f1 * g:half + (f1 + 1) * g, lanes]
                a_ref[hi, f1 * g:(f1 + 1) * g, :] = re * c - im * s
                a_ref[hi, half + f1 * g:half + (f1 + 1) * g, :] = re * s + im * c
        return carry

    lax.fori_loop(0, n2 // g // LANE_PAIR, stage_a, 0, unroll=2)

    for f0 in range(0, nf, LANE_PAIR):
        zs = []
        for f1 in range(f0, f0 + LANE_PAIR):
            ar = a_ref[:, f1 * g:(f1 + 1) * g, :].reshape(n2, cw)
            ai = a_ref[:, half + f1 * g:half + (f1 + 1) * g, :].reshape(n2, cw)
            zs.append(jnp.concatenate([ar, ai], axis=0).astype(BF16))
        emit(f0, jnp.concatenate(zs, axis=1))


def _filter_two_stage_body(hid_ref, dist_ref, delt_ref, w3_ref, ma_ref, fb_ref, twc_ref, tws_ref,
                           o_ref, h_ref, a_ref, *, n1):
    h = _implicit_filter(hid_ref, dist_ref, delt_ref, w3_ref)
    h_ref[...] = h.reshape(h_ref.shape)
    inv_n = 1.0 / (n1 * DFT_INNER)

    cw = h_ref.shape[-1]

    def emit(f1, z):
        spec = (_dot(fb_ref[...], z) * inv_n).astype(o_ref.dtype)
        for j in range(LANE_PAIR):
            o_ref[f1 + j] = spec[:, j * cw:(j + 1) * cw]

    _forward_two_stage(h_ref, a_ref, ma_ref, twc_ref, tws_ref, n1, emit)


def _filter_one_stage_body(hid_ref, dist_ref, delt_ref, w3_ref, fwd_ref, o_ref):
    h = _implicit_filter(hid_ref, dist_ref, delt_ref, w3_ref)
    inv_n = 1.0 / fwd_ref.shape[0] * 2.0
    o_ref[...] = (_dot(fwd_ref[...], h.astype(BF16)) * inv_n).astype(o_ref.dtype)


def _filter_input_specs(seq, cw):
    return [
        pl.BlockSpec((None, seq, FILT_HID), lambda l, j: (l, 0, 0)),
        pl.BlockSpec((seq, LANES), lambda l, j: (0, 0)),
        pl.BlockSpec((1, cw), lambda l, j: (0, j)),
        pl.BlockSpec((None, FILT_HID, cw), lambda l, j: (l, 0, j)),
    ]


def _filter_weights(hy_f_w1, hy_f_b1, hy_f_w2, hy_f_b2, hy_f_w3, hy_f_freq):
    depth = hy_f_w1.shape[0]
    w1 = jnp.pad(hy_f_w1, ((0, 0), (0, FILT_HID - FILT_EMB), (0, 0)))
    return (w1, hy_f_b1.reshape(depth, 1, FILT_HID), hy_f_w2, hy_f_b2.reshape(depth, 1, FILT_HID),
            hy_f_w3, hy_f_freq.reshape(depth, 1, FILT_HID))


def _filter_spectra_two_stage(seq, fweights):
    depth = fweights[0].shape[0]
    cw = LANES
    n2 = DFT_INNER
    n1 = 2 * seq // n2
    _, dist, deltas = _filter_constants(seq)
    ma, fb, _, _, twc, tws = _two_stage_tables(seq)
    nf = twc.shape[1]
    ncol = HY_ORDER * HY_C
    return pl.pallas_call(
        functools.partial(_filter_two_stage_body, n1=n1),
        grid=(depth, ncol // cw),
        in_specs=[
            *_filter_input_specs(seq, cw),
            pl.BlockSpec(ma.shape, lambda l, j: (0, 0)),
            pl.BlockSpec(fb.shape, lambda l, j: (0, 0)),
            pl.BlockSpec(twc.shape, lambda l, j: (0, 0, 0)),
            pl.BlockSpec(tws.shape, lambda l, j: (0, 0, 0)),
        ],
        out_specs=pl.BlockSpec((None, nf, 2 * n2, cw), lambda l, j: (l, 0, 0, j)),
        out_shape=jax.ShapeDtypeStruct((depth, nf, 2 * n2, ncol), BF16),
        scratch_shapes=[pltpu.VMEM((n1 // 2, n2, cw), F32),
                        pltpu.VMEM((n2 // ROW_GROUP, 2 * nf * ROW_GROUP, cw), F32)],
        compiler_params=_params("arbitrary", "arbitrary"),
        name="hyena_filter_long",
    )(_filter_hidden(seq, fweights), dist, deltas, fweights[4], ma, fb, twc, tws)


def _filter_spectra_one_stage(seq, fweights):
    depth = fweights[0].shape[0]
    cw = LANES
    _, dist, deltas = _filter_constants(seq)
    fwd, _ = _one_stage_tables(seq)
    ncol = HY_ORDER * HY_C
    return pl.pallas_call(
        _filter_one_stage_body,
        grid=(depth, ncol // cw),
        in_specs=[*_filter_input_specs(seq, cw), pl.BlockSpec(fwd.shape, lambda l, j: (0, 0))],
        out_specs=pl.BlockSpec((None, 4 * seq, cw), lambda l, j: (l, 0, j)),
        out_shape=jax.ShapeDtypeStruct((depth, 4 * seq, ncol), BF16),
        compiler_params=_params("arbitrary", "arbitrary"),
        name="hyena_filter_short",
    )(_filter_hidden(seq, fweights), dist, deltas, fweights[4], fwd)


def _short_conv(x, w_ref, b_ref, part):
    seq = x.shape[0]
    rows = lax.broadcasted_iota(jnp.int32, x.shape, 0)
    prev = jnp.where(rows == 0, 0.0, pltpu.roll(x, 1, 0))
    nxt = jnp.where(rows == seq - 1, 0.0, pltpu.roll(x, seq - 1, 0))
    w = w_ref[part]
    return prev * w[0:1] + x * w[1:2] + nxt * w[2:3] + b_ref[part:part + 1, :]


def _complex_mul(x, h, half):
    xr, xi = x[:half], x[half:]
    hr, hi = h[:half].astype(F32), h[half:].astype(F32)
    return jnp.concatenate([xr * hr - xi * hi, xr * hi + xi * hr], axis=0)


def _hyena_long_body(v_ref, x1_ref, x2_ref, g_ref, cw_ref, cb_ref, skip_ref, hf0_ref, hf1_ref,
                     ma_ref, fb_ref, fbi_ref, mc_ref, twc_ref, tws_ref, o_ref, y_ref, c_ref, a_ref, w_ref,
                     *, n1):
    n2 = DFT_INNER
    g = ROW_GROUP
    h1 = n1 // 2
    seq, cw = v_ref.shape
    nf = twc_ref.shape[1]
    half = nf * g

    def long_conv(hf_ref):
        def emit(f1, z):
            hf = jnp.concatenate([hf_ref[f1 + j] for j in range(LANE_PAIR)], axis=1)
            spec = _complex_mul(_dot(fb_ref[...], z), hf, n2)
            w = _dot(fbi_ref[...], spec.astype(BF16))
            for j in range(LANE_PAIR):
                r0 = (f1 + j) * g
                r1 = half + (f1 + j) * g
                w_ref[:, r0:r0 + g, :] = w[:n2, j * cw:(j + 1) * cw].reshape(n2 // g, g, cw)
                w_ref[:, r1:r1 + g, :] = w[n2:, j * cw:(j + 1) * cw].reshape(n2 // g, g, cw)

        _forward_two_stage(y_ref, a_ref, ma_ref, twc_ref, tws_ref, n1, emit)

        def stage_c(step, carry):
            his = [step * LANE_PAIR + j for j in range(LANE_PAIR)]
            cols = []
            for hi in his:
                res, ims = [], []
                twc = twc_ref[hi]
                tws = tws_ref[hi]
                for f1 in range(nf):
                    c = twc[f1:f1 + 1]
                    s = tws[f1:f1 + 1]
                    re = w_ref[hi, f1 * g:(f1 + 1) * g, :]
                    im = w_ref[hi, half + f1 * g:half + (f1 + 1) * g, :]
                    res.append(re * c + im * s)
                    ims.append(im * c - re * s)
                cols.append(jnp.concatenate(res + ims, axis=0).astype(BF16))
            y = _dot(mc_ref[...], jnp.concatenate(cols, axis=1))
            for j, hi in enumerate(his):
                c_ref[:, pl.ds(pl.multiple_of(hi * g, g), g), :] = y[:, j * cw:(j + 1) * cw].reshape(h1, g, cw)
            return carry

        lax.fori_loop(0, n2 // g // LANE_PAIR, stage_c, 0, unroll=2)

    def load3(ref):
        return ref[...].reshape(seq, cw)

    y_ref[...] = _short_conv(v_ref[...], cw_ref, cb_ref, 0).reshape(y_ref.shape)
    long_conv(hf0_ref)
    y = _short_conv(x1_ref[...], cw_ref, cb_ref, 1) * (load3(c_ref) + skip_ref[0:1, :] * load3(y_ref))
    y_ref[...] = y.reshape(y_ref.shape)
    long_conv(hf1_ref)
    out = _short_conv(x2_ref[...], cw_ref, cb_ref, 2) * (load3(c_ref) + skip_ref[1:2, :] * load3(y_ref))
    o_ref[...] = (out * _silu(g_ref[...])).astype(o_ref.dtype)


def _const_spec(shape):
    nd = len(shape)
    return pl.BlockSpec(shape, lambda *_: (0,) * nd, pipeline_mode=pl.Buffered(1))


def _hyena_long(p, conv_w, conv_b, skip, spectra, layer):
    b, seq, _ = p.shape
    cw = LANES
    n2 = DFT_INNER
    n1 = 2 * seq // n2
    nblk = HY_C // cw
    ma, fb, fbi, mc, twc, tws = _two_stage_tables(seq)
    nf = twc.shape[1]

    def pcol(off):
        return pl.BlockSpec((None, seq, cw), lambda j, i: (i, 0, off // cw + j))

    def hf_spec(order):
        return pl.BlockSpec((None, nf, 2 * n2, cw), lambda j, i: (layer, 0, 0, order * nblk + j),
                            pipeline_mode=pl.Buffered(1))

    stage_rows = (n2 // ROW_GROUP, 2 * nf * ROW_GROUP, cw)
    return pl.pallas_call(
        functools.partial(_hyena_long_body, n1=n1),
        grid=(nblk, b),
        in_specs=[
            pcol(OFF_HYZ), pcol(OFF_HYZ + HY_C), pcol(OFF_HYZ + 2 * HY_C), pcol(OFF_HYG),
            pl.BlockSpec((None, HY_ORDER + 1, SHORT_K, cw), lambda j, i: (layer, 0, 0, j)),
            pl.BlockSpec((None, HY_ORDER + 1, cw), lambda j, i: (layer, 0, j)),
            pl.BlockSpec((None, HY_ORDER, cw), lambda j, i: (layer, 0, j)),
            hf_spec(0), hf_spec(1),
            _const_spec(ma.shape), _const_spec(fb.shape), _const_spec(fbi.shape), _const_spec(mc.shape),
            _const_spec(twc.shape), _const_spec(tws.shape),
        ],
        out_specs=pl.BlockSpec((None, seq, cw), lambda j, i: (i, 0, j)),
        out_shape=jax.ShapeDtypeStruct((b, seq, HY_C), BF16),
        scratch_shapes=[
            pltpu.VMEM((n1 // 2, n2, cw), F32), pltpu.VMEM((n1 // 2, n2, cw), F32),
            pltpu.VMEM(stage_rows, F32), pltpu.VMEM(stage_rows, F32),
        ],
        compiler_params=_params("arbitrary", "arbitrary"),
        name="hyena_long",
    )(p, p, p, p, conv_w, conv_b, skip, spectra, spectra, ma, fb, fbi, mc, twc, tws)


def _hyena_short_body(v_ref, x1_ref, x2_ref, g_ref, cw_ref, cb_ref, skip_ref, hf0_ref, hf1_ref,
                      fwd_ref, inv_ref, o_ref):
    n = fwd_ref.shape[0] // 2

    def long_conv(y, hf_ref):
        spec = _complex_mul(_dot(fwd_ref[...], y.astype(BF16)), hf_ref[...], n)
        return _dot(inv_ref[...], spec.astype(BF16))

    y = _short_conv(v_ref[...], cw_ref, cb_ref, 0)
    y = _short_conv(x1_ref[...], cw_ref, cb_ref, 1) * (long_conv(y, hf0_ref) + skip_ref[0:1, :] * y)
    y = _short_conv(x2_ref[...], cw_ref, cb_ref, 2) * (long_conv(y, hf1_ref) + skip_ref[1:2, :] * y)
    o_ref[...] = (y * _silu(g_ref[...])).astype(o_ref.dtype)


def _hyena_short(p, conv_w, conv_b, skip, spectra, layer):
    b, seq, _ = p.shape
    cw = LANE_PAIR * LANES
    nblk = HY_C // cw
    fwd, inv = _one_stage_tables(seq)

    def pcol(off):
        return pl.BlockSpec((None, seq, cw), lambda j, i: (i, 0, off // cw + j))

    def hf_spec(order):
        return pl.BlockSpec((None, 4 * seq, cw), lambda j, i: (layer, 0, order * nblk + j))

    return pl.pallas_call(
        _hyena_short_body,
        grid=(nblk, b),
        in_specs=[
            pcol(OFF_HYZ), pcol(OFF_HYZ + HY_C), pcol(OFF_HYZ + 2 * HY_C), pcol(OFF_HYG),
            pl.BlockSpec((None, HY_ORDER + 1, SHORT_K, cw), lambda j, i: (layer, 0, 0, j)),
            pl.BlockSpec((None, HY_ORDER + 1, cw), lambda j, i: (layer, 0, j)),
            pl.BlockSpec((None, HY_ORDER, cw), lambda j, i: (layer, 0, j)),
            hf_spec(0), hf_spec(1),
            pl.BlockSpec(fwd.shape, lambda j, i: (0, 0)), pl.BlockSpec(inv.shape, lambda j, i: (0, 0)),
        ],
        out_specs=pl.BlockSpec((None, seq, cw), lambda j, i: (i, 0, j)),
        out_shape=jax.ShapeDtypeStruct((b, seq, HY_C), BF16),
        compiler_params=_params("arbitrary", "arbitrary"),
        name="hyena_short",
    )(p, p, p, p, conv_w, conv_b, skip, spectra, spectra, fwd, inv)


def _chunk_mlp_body(u_ref, v_ref, g_ref, lg_ref, lb_ref, ws_ref, bs_ref, o_ref):
    v = v_ref[...]
    mu = jnp.mean(v, axis=-1, keepdims=True)
    vc = v - mu
    var = jnp.mean(vc * vc, axis=-1, keepdims=True)
    vn = (vc * lax.rsqrt(var + EPS) * lg_ref[...] + lb_ref[...]).astype(BF16)
    rows = v.shape[0]
    for n in range(rows // CHUNK):
        r = slice(n * CHUNK, (n + 1) * CHUNK)
        for h in range(CM_HEADS):
            c = slice(h * CM_HD, (h + 1) * CM_HD)
            s = _dot(ws_ref[h].astype(BF16), vn[r, c]) + bs_ref[:, c]
            o_ref[r, c] = (u_ref[r, c] * s * _silu(g_ref[r, c])).astype(o_ref.dtype)


def _chunk_mlp(p, ln_g, ln_b, w_s, b_s, layer):
    b, seq, _ = p.shape
    bm = min(seq, 512)
    bs_full = jnp.repeat(jnp.swapaxes(b_s, 1, 2), CM_HD, axis=2)

    def pcol(off):
        return pl.BlockSpec((None, bm, CM_C), lambda i, t: (i, t, off // CM_C))

    return pl.pallas_call(
        _chunk_mlp_body,
        grid=(b, seq // bm),
        in_specs=[
            pcol(OFF_CMU), pcol(OFF_CMV), pcol(OFF_CMG),
            pl.BlockSpec((None, 1, CM_C), lambda i, t: (layer, 0, 0)),
            pl.BlockSpec((None, 1, CM_C), lambda i, t: (layer, 0, 0)),
            pl.BlockSpec((None, CM_HEADS, CHUNK, CHUNK), lambda i, t: (layer, 0, 0, 0)),
            pl.BlockSpec((None, CHUNK, CM_C), lambda i, t: (layer, 0, 0)),
        ],
        out_specs=pl.BlockSpec((None, bm, CM_C), lambda i, t: (i, t, 0)),
        out_shape=jax.ShapeDtypeStruct((b, seq, CM_C), BF16),
        compiler_params=_params("arbitrary", "arbitrary"),
        name="chunk_mlp",
    )(p, p, p, ln_g.reshape(-1, 1, CM_C), ln_b.reshape(-1, 1, CM_C), w_s, bs_full)


def _rms_heads(x, g, nheads):
    outs = []
    for h in range(nheads):
        xh = x[:, h * HEAD_DIM:(h + 1) * HEAD_DIM]
        ms = jnp.mean(xh * xh, axis=-1, keepdims=True)
        outs.append(xh * lax.rsqrt(ms + EPS) * g)
    return outs


def _rope(xh, cos, sin):
    lane = lax.broadcasted_iota(jnp.int32, xh.shape, 1)
    quarter = ROPE_AXIS // 2
    partner = jnp.where((lane % ROPE_AXIS) < quarter,
                        pltpu.roll(xh, HEAD_DIM - quarter, 1), pltpu.roll(xh, quarter, 1))
    return xh * cos + partner * sin


def _kv_prep_body(k_ref, v_ref, gk_ref, cos_ref, sin_ref, kn_ref, kr_ref, vt_ref, *, rotary):
    heads = _rms_heads(k_ref[...], gk_ref[...], N_KV_HEADS)
    for h, kh in enumerate(heads):
        c = slice(h * HEAD_DIM, (h + 1) * HEAD_DIM)
        kn_ref[:, c] = kh
        if rotary:
            kh = _rope(kh, cos_ref[...], sin_ref[...])
        kr_ref[:, c] = kh.astype(BF16)
        vt_ref[h, 0:HEAD_DIM, :] = v_ref[:, c].T.astype(BF16)
        vt_ref[h, HEAD_DIM:, :] = jnp.ones((ATT_ONES_ROWS, v_ref.shape[0]), BF16)


def _kv_prep(p, k_norm_g, cos, sin, layer, rotary):
    b, seq, _ = p.shape
    bm = min(seq, 512)
    blk = pl.BlockSpec((None, bm, KV_C), lambda i, t: (i, t, 0))
    vt_blk = pl.BlockSpec((None, N_KV_HEADS, HEAD_DIM + ATT_ONES_ROWS, bm), lambda i, t: (i, 0, 0, t))
    return pl.pallas_call(
        functools.partial(_kv_prep_body, rotary=rotary),
        grid=(b, seq // bm),
        in_specs=[
            pl.BlockSpec((None, bm, KV_C), lambda i, t: (i, t, OFF_K // KV_C)),
            pl.BlockSpec((None, bm, KV_C), lambda i, t: (i, t, OFF_V // KV_C)),
            pl.BlockSpec((None, 1, HEAD_DIM), lambda i, t: (layer, 0, 0)),
            pl.BlockSpec((bm, HEAD_DIM), lambda i, t: (t, 0)),
            pl.BlockSpec((bm, HEAD_DIM), lambda i, t: (t, 0)),
        ],
        out_specs=[blk, blk, vt_blk],
        out_shape=[jax.ShapeDtypeStruct((b, seq, KV_C), F32),
                   jax.ShapeDtypeStruct((b, seq, KV_C), BF16),
                   jax.ShapeDtypeStruct((b, N_KV_HEADS, HEAD_DIM + ATT_ONES_ROWS, seq), BF16)],
        compiler_params=_params("arbitrary", "arbitrary"),
        name="kv_prep",
    )(p, p, k_norm_g.reshape(-1, 1, HEAD_DIM), cos, sin)


def _attention_body(*refs, rotary, has_ctx):
    q_ref, g_ref, gq_ref, cos_ref, sin_ref, k_ref, vt_ref = refs[:7]
    if has_ctx:
        ck_ref, cv_ref, o_ref, s_ref, p_ref = refs[7:]
    else:
        o_ref, s_ref, p_ref = refs[7:]
    nslot, s_all, nq = s_ref.shape
    tq = nq // GQA
    nsub = q_ref.shape[0] // tq
    s_self = k_ref.shape[0]
    chunk = min(ATT_KEY_CHUNK, s_self)
    sub = ROW_GROUP
    scale = math.log2(math.e) / math.sqrt(HEAD_DIM)
    contract_last = (((1,), (1,)), ((), ()))
    key_blocks = [slice(j * chunk, (j + 1) * chunk) for j in range(s_self // chunk)]
    if has_ctx:
        ck = ck_ref[...].astype(BF16)
        cvt = jnp.concatenate([cv_ref[...].T.astype(BF16),
                               jnp.ones((ATT_ONES_ROWS, cv_ref.shape[0]), BF16)], axis=0)

    def col_reduce(x, op):
        return op(x.reshape(x.shape[0] // sub, sub, nq), axis=0)

    blocks = key_blocks + ([slice(s_self, s_all)] if has_ctx else [])

    def scores(t):
        rows_q = slice(t * tq, (t + 1) * tq)
        heads = _rms_heads(q_ref[rows_q, :], gq_ref[...], GQA)
        if rotary:
            heads = [_rope(qh, cos_ref[rows_q, :], sin_ref[rows_q, :]) for qh in heads]
        q = jnp.concatenate([(qh * scale).astype(BF16) for qh in heads], axis=0)
        slot = t % nslot
        m8 = None
        for i, rows in enumerate(blocks):
            kb = ck if (has_ctx and i == len(blocks) - 1) else k_ref[rows, :]
            st = lax.dot_general(kb, q, contract_last, preferred_element_type=F32)
            s_ref[slot, rows, :] = st
            part = col_reduce(st, jnp.max)
            m8 = part if m8 is None else jnp.maximum(m8, part)
            yield None if i < len(blocks) - 1 else jnp.max(m8, axis=0, keepdims=True)

    def weights(t, m):
        slot = t % nslot
        for rows in blocks:
            p_ref[slot, rows, :] = jnp.exp2(s_ref[slot, rows, :] - m).astype(BF16)
            yield None

    def values(t):
        slot = t % nslot
        rows_q = slice(t * tq, (t + 1) * tq)
        ot = None
        for i, rows in enumerate(blocks):
            vtb = cvt if (has_ctx and i == len(blocks) - 1) else vt_ref[:, rows]
            part = _dot(vtb, p_ref[slot, rows, :])
            ot = part if ot is None else ot + part
            if i == len(blocks) - 1:
                ot = ot[0:HEAD_DIM] / ot[HEAD_DIM:HEAD_DIM + 1]
                for h in range(GQA):
                    c = slice(h * HEAD_DIM, (h + 1) * HEAD_DIM)
                    oh = ot[:, h * tq:(h + 1) * tq].T
                    o_ref[rows_q, c] = (oh * _silu(g_ref[rows_q, c])).astype(o_ref.dtype)
            yield None

    col_max = {}
    for phase in range(nsub + 2):
        stages = []
        if phase < nsub:
            stages.append(("scores", scores(phase)))
        if 0 <= phase - 1 < nsub:
            stages.append(("weights", weights(phase - 1, col_max[phase - 1])))
        if 0 <= phase - 2 < nsub:
            stages.append(("values", values(phase - 2)))
        for _ in blocks:
            for name, steps in stages:
                out = next(steps)
                if name == "scores":
                    col_max[phase] = out


def _attention(p, q_norm_g, cos, sin, keys, values_t, ctx_k, ctx_v, layer, rotary, tq):
    b, seq, _ = p.shape
    skv = keys.shape[1]
    gw = GQA * HEAD_DIM
    has_ctx = ctx_k is not None
    in_specs = [
        pl.BlockSpec((None, tq, gw), lambda i, h, t: (i, t, OFF_Q // gw + h)),
        pl.BlockSpec((None, tq, gw), lambda i, h, t: (i, t, OFF_ATG // gw + h)),
        pl.BlockSpec((None, 1, HEAD_DIM), lambda i, h, t: (layer, 0, 0)),
        pl.BlockSpec((tq, HEAD_DIM), lambda i, h, t: (t, 0)),
        pl.BlockSpec((tq, HEAD_DIM), lambda i, h, t: (t, 0)),
        pl.BlockSpec((None, skv, HEAD_DIM), lambda i, h, t: (i, 0, h)),
        pl.BlockSpec((None, None, HEAD_DIM + ATT_ONES_ROWS, skv), lambda i, h, t: (i, h, 0, 0)),
    ]
    args = [p, p, q_norm_g.reshape(-1, 1, HEAD_DIM), cos, sin, keys, values_t]
    s_all = skv
    if has_ctx:
        past = ctx_k.shape[2]
        spec = pl.BlockSpec((None, None, past, HEAD_DIM), lambda i, h, t: (i, layer, 0, h))
        in_specs += [spec, spec]
        args += [ctx_k, ctx_v]
        s_all += past
    slots = (ATT_PIPE_SLOTS, s_all, GQA * ATT_SUBTILE_ROWS)
    return pl.pallas_call(
        functools.partial(_attention_body, rotary=rotary, has_ctx=has_ctx),
        grid=(b, N_KV_HEADS, seq // tq),
        in_specs=in_specs,
        out_specs=pl.BlockSpec((None, tq, gw), lambda i, h, t: (i, t, h)),
        out_shape=jax.ShapeDtypeStruct((b, seq, ATT_C), BF16),
        scratch_shapes=[pltpu.VMEM(slots, F32), pltpu.VMEM(slots, BF16)],
        compiler_params=_params("arbitrary", "arbitrary", "arbitrary"),
        name="attention",
    )(*args)


def _out_proj_body(yh_ref, yc_ref, ya_ref, x_ref, mod_ref, w_ref, lg_ref, lb_ref, o_ref):
    d = x_ref.shape[-1]
    gate = mod_ref[:, 2 * d:3 * d]
    mixed = jnp.concatenate([yh_ref[...], yc_ref[...], ya_ref[...]], axis=1)
    r = DN_ALPHA * x_ref[...] + gate * _dot(mixed, w_ref[...])
    mu = jnp.mean(r, axis=-1, keepdims=True)
    rc = r - mu
    var = jnp.mean(rc * rc, axis=-1, keepdims=True)
    o_ref[...] = rc * lax.rsqrt(var + EPS) * lg_ref[...] + lb_ref[...]


def _out_proj(y_hy, y_cm, y_at, x, mod, row0, w_out_bf, ln_g, ln_b, layer):
    g, t, d = x.shape
    bm = 512

    def rows(c):
        return pl.BlockSpec((None, bm, c), lambda b, i: (b, i, 0))

    return pl.pallas_call(
        _out_proj_body,
        grid=(g, t // bm),
        in_specs=[
            rows(HY_C), rows(CM_C), rows(ATT_C), rows(d),
            pl.BlockSpec((None, None, 1, 3 * d), lambda b, i: (layer, row0 + b, 0, 0)),
            pl.BlockSpec((None, D_MIX, d), lambda b, i: (layer, 0, 0)),
            pl.BlockSpec((None, 1, d), lambda b, i: (layer, 0, 0)),
            pl.BlockSpec((None, 1, d), lambda b, i: (layer, 0, 0)),
        ],
        out_specs=rows(d),
        out_shape=jax.ShapeDtypeStruct((g, t, d), F32),
        compiler_params=_params("arbitrary", "arbitrary"),
        name="out_proj_ln",
    )(y_hy, y_cm, y_at, x, mod, w_out_bf, ln_g.reshape(-1, 1, d), ln_b.reshape(-1, 1, d))


@functools.lru_cache(maxsize=None)
def _rope_tables(seq):
    rows = seq // GRID_W
    row = np.repeat(np.arange(rows, dtype=np.float32), GRID_W)
    col = np.tile(np.arange(GRID_W, dtype=np.float32), rows)
    inv = (np.float32(ROPE_THETA) ** (-np.arange(0, ROPE_AXIS, 2, dtype=np.float32) / np.float32(ROPE_AXIS))
           ).astype(np.float32)
    ra = row[:, None] * inv[None, :]
    ca = col[:, None] * inv[None, :]
    cos = np.concatenate([np.cos(ra), np.cos(ra), np.cos(ca), np.cos(ca)], axis=1)
    sin = np.concatenate([-np.sin(ra), np.sin(ra), -np.sin(ca), np.sin(ca)], axis=1)
    return jnp.asarray(cos, F32), jnp.asarray(sin, F32)


def _mixer_layer(x, seq, mod, row0, layer, rotary, w_in_bf, w_out_bf, spectra, hy_conv_w, hy_conv_b, hy_skip,
                 cm_ln_g, cm_ln_b, cm_w_s, cm_b_s, q_norm_g, k_norm_g, ln_g, ln_b, ctx_k, ctx_v, cos, sin):
    g, t, d = x.shape
    nreq = g * t // seq
    p = _in_proj(x, mod, row0, w_in_bf, layer).reshape(nreq, seq, D_IN)
    if rotary:
        y_hy = _hyena_long(p, hy_conv_w, hy_conv_b, hy_skip, spectra, layer)
    else:
        y_hy = _hyena_short(p, hy_conv_w, hy_conv_b, hy_skip, spectra, layer)
    y_cm = _chunk_mlp(p, cm_ln_g, cm_ln_b, cm_w_s, cm_b_s, layer)
    k_normed, keys, values = _kv_prep(p, k_norm_g, cos, sin, layer, rotary)
    y_at = _attention(p, q_norm_g, cos, sin, keys, values, ctx_k, ctx_v, layer, rotary,
                      tq=min(seq, ATT_SUBTILES * ATT_SUBTILE_ROWS))
    x_new = _out_proj(y_hy.reshape(g, t, HY_C), y_cm.reshape(g, t, CM_C), y_at.reshape(g, t, ATT_C),
                      x, mod, row0, w_out_bf, ln_g, ln_b, layer)
    return x_new, k_normed, p


def kernel(x_prompt, x_sample, cache_k, cache_v, c, c_ctx, w_mod, b_mod, w_in, hy_conv_w, hy_conv_b, hy_f_w1, hy_f_b1, hy_f_w2, hy_f_b2, hy_f_w3, hy_f_freq, hy_skip, cm_ln_g, cm_ln_b, cm_w_s, cm_b_s, q_norm_g, k_norm_g, w_out, ln_g, ln_b):
    batch, seq, d = x_prompt.shape
    dec_batch, dec_seq, _ = x_sample.shape
    depth = w_in.shape[0]
    past = cache_k.shape[2]

    cond = jnp.concatenate([c_ctx[None, :], c, jnp.zeros((MOD_ROWS - 1 - dec_batch, d), F32)], axis=0)
    mod = _modulation(cond, w_mod, b_mod).reshape(depth, MOD_ROWS, 1, 3 * d)

    w_in_bf = w_in.astype(BF16)
    w_out_bf = w_out.astype(BF16)
    fweights = _filter_weights(hy_f_w1, hy_f_b1, hy_f_w2, hy_f_b2, hy_f_w3, hy_f_freq)
    spectra_ctx = _filter_spectra_one_stage(seq, fweights)
    spectra_lat = _filter_spectra_two_stage(dec_seq, fweights)
    conv_w = hy_conv_w.reshape(depth, SHORT_K, HY_ORDER + 1, HY_C).transpose(0, 2, 1, 3)
    conv_b = hy_conv_b.reshape(depth, HY_ORDER + 1, HY_C)
    cos, sin = _rope_tables(dec_seq)
    ctx_k = cache_k.reshape(dec_batch, depth, past, KV_C)
    ctx_v = cache_v.reshape(dec_batch, depth, past, KV_C)

    xp = x_prompt.reshape(1, batch * seq, d)
    xs = x_sample
    new_k, new_v = [], []
    for layer in range(depth):
        common = dict(layer=layer, w_in_bf=w_in_bf, w_out_bf=w_out_bf, hy_conv_w=conv_w, hy_conv_b=conv_b,
                      hy_skip=hy_skip, cm_ln_g=cm_ln_g, cm_ln_b=cm_ln_b, cm_w_s=cm_w_s, cm_b_s=cm_b_s,
                      q_norm_g=q_norm_g, k_norm_g=k_norm_g, ln_g=ln_g, ln_b=ln_b, cos=cos, sin=sin)
        xp, k_l, p_ctx = _mixer_layer(xp, seq, mod, 0, rotary=False, spectra=spectra_ctx,
                                      ctx_k=None, ctx_v=None, **common)
        new_k.append(k_l.reshape(batch, seq, N_KV_HEADS, HEAD_DIM))
        new_v.append(p_ctx[:, :, OFF_V:OFF_V + KV_C].reshape(batch, seq, N_KV_HEADS, HEAD_DIM))
        xs, _, _ = _mixer_layer(xs, dec_seq, mod, 1, rotary=True, spectra=spectra_lat,
                                ctx_k=ctx_k, ctx_v=ctx_v, **common)
    return (xp.reshape(batch, seq, d), xs, jnp.stack(new_k, axis=1), jnp.stack(new_v, axis=1))
```

```python
import functools
import math

import numpy as np
import jax
import jax.numpy as jnp
from jax import lax
from jax.experimental import pallas as pl
from jax.experimental.pallas import tpu as pltpu

F32 = jnp.float32
BF16 = jnp.bfloat16

D_MODEL = 2048
DEPTH = 4
GRID_W = 64
HY_C = 512
HY_ORDER = 2
SHORT_K = 3
FILT_EMB = 33
FILT_HID = 64
HY_FAST_DECAY = 0.3
HY_SLOW_DECAY = 1.5
HY_TARGET = 1e-2
CM_HEADS = 4
CM_HD = 128
CM_C = CM_HEADS * CM_HD
CHUNK = 128
N_HEADS = 8
N_KV_HEADS = 2
HEAD_DIM = 128
GQA = N_HEADS // N_KV_HEADS
ATT_C = N_HEADS * HEAD_DIM
KV_C = N_KV_HEADS * HEAD_DIM
ROPE_THETA = 10000.0
ROPE_AXIS = HEAD_DIM // 2
D_MIX = HY_C + CM_C + ATT_C
EPS = 1e-6
DN_ALPHA = (2 * DEPTH) ** 0.25

OFF_HYZ = 0
OFF_HYG = (HY_ORDER + 1) * HY_C
OFF_CMU = OFF_HYG + HY_C
OFF_CMV = OFF_CMU + CM_C
OFF_CMG = OFF_CMV + CM_C
OFF_Q = OFF_CMG + CM_C
OFF_K = OFF_Q + ATT_C
OFF_V = OFF_K + KV_C
OFF_ATG = OFF_V + KV_C
D_IN = OFF_ATG + ATT_C

LANES = 128
VMEM_LIMIT_BYTES = 56 * 1024 * 1024

DFT_INNER = 128
ROW_GROUP = 8
LANE_PAIR = 2
ATT_KEY_CHUNK = 512
ATT_SUBTILE_ROWS = 128
ATT_ONES_ROWS = 16
ATT_PIPE_SLOTS = 2
ATT_SUBTILES = 8
MOD_ROWS = 16


def _params(*semantics):
    return pltpu.CompilerParams(dimension_semantics=semantics, vmem_limit_bytes=VMEM_LIMIT_BYTES)


def _silu(x):
    return x * jax.nn.sigmoid(x)


def _dot(a, b):
    return jnp.dot(a, b, preferred_element_type=F32)


def _dot_exact(a, b):
    return jnp.dot(a, b, preferred_element_type=F32, precision=lax.Precision.HIGHEST)


def _mod_body(c_ref, w_ref, b_ref, o_ref):
    s = _silu(c_ref[...]).astype(BF16)
    o_ref[...] = _dot(s, w_ref[...].astype(BF16)) + b_ref[...]


def _modulation(cond, w_mod, b_mod):
    depth, d, d3 = w_mod.shape
    bn = 1536
    return pl.pallas_call(
        _mod_body,
        grid=(depth, d3 // bn),
        in_specs=[
            pl.BlockSpec((MOD_ROWS, d), lambda l, j: (0, 0)),
            pl.BlockSpec((None, d, bn), lambda l, j: (l, 0, j)),
            pl.BlockSpec((None, 1, bn), lambda l, j: (l, 0, j)),
        ],
        out_specs=pl.BlockSpec((None, MOD_ROWS, bn), lambda l, j: (l, 0, j)),
        out_shape=jax.ShapeDtypeStruct((depth, MOD_ROWS, d3), F32),
        compiler_params=_params("arbitrary", "arbitrary"),
        name="modulation",
    )(cond, w_mod, b_mod.reshape(depth, 1, d3))


def _in_proj_body(x_ref, mod_ref, w_ref, o_ref):
    d = x_ref.shape[-1]
    shift = mod_ref[:, 0:d]
    scale = mod_ref[:, d:2 * d]
    h = (x_ref[...] * (1.0 + scale) + shift).astype(BF16)
    o_ref[...] = _dot(h, w_ref[...])


def _in_proj(x, mod, row0, w_in_bf, layer):
    g, t, d = x.shape
    d_in = w_in_bf.shape[-1]
    bm, bn = 1024, 1024
    return pl.pallas_call(
        _in_proj_body,
        grid=(g, t // bm, d_in // bn),
        in_specs=[
            pl.BlockSpec((None, bm, d), lambda b, i, j: (b, i, 0)),
            pl.BlockSpec((None, None, 1, 3 * d), lambda b, i, j: (layer, row0 + b, 0, 0)),
            pl.BlockSpec((None, d, bn), lambda b, i, j: (layer, 0, j)),
        ],
        out_specs=pl.BlockSpec((None, bm, bn), lambda b, i, j: (b, i, j)),
        out_shape=jax.ShapeDtypeStruct((g, t, d_in), F32),
        compiler_params=_params("arbitrary", "arbitrary", "arbitrary"),
        name="in_proj",
    )(x, mod, w_in_bf)


def _cs(k, n):
    ang = 2.0 * np.pi * (np.asarray(k, dtype=np.int64) % n).astype(np.float64) / n
    return np.cos(ang), np.sin(ang)


@functools.lru_cache(maxsize=None)
def _two_stage_tables(seq):
    n = 2 * seq
    n2 = DFT_INNER
    n1 = n // n2
    h1 = n1 // 2
    g = ROW_GROUP
    nf = -(-(n1 // 2 + 1) // LANE_PAIR) * LANE_PAIR
    t1 = np.arange(h1)
    f1 = np.arange(nf)
    live = (f1 <= n1 // 2).astype(np.float64)
    mirror_weight = np.where((f1 == 0) | (f1 == n1 // 2), 1.0, 2.0) * live
    t2 = np.arange(n2)
    f2 = np.arange(n2)
    lo = np.arange(g)
    eye = np.eye(g)
    k = t1[None, None, :, None] * f1[:, None, None, None] * n2 + lo[None, None, None, :] * f1[:, None, None, None]
    c, s = _cs(k, n)
    c = c * eye[None, :, None, :] * live[:, None, None, None]
    s = s * eye[None, :, None, :] * live[:, None, None, None]
    ma = np.concatenate([c.reshape(nf * g, h1 * g), -s.reshape(nf * g, h1 * g)], axis=0)
    hi = np.arange(n2 // g)
    c, s = _cs(g * hi[:, None] * f1[None, :], n)
    twc = np.broadcast_to(c[:, :, None], c.shape + (LANES,))
    tws = np.broadcast_to(-s[:, :, None], s.shape + (LANES,))
    c, s = _cs(f2[:, None] * t2[None, :], n2)
    fb = np.block([[c, s], [-s, c]])
    c, s = _cs(t2[:, None] * f2[None, :], n2)
    fbi = np.block([[c, -s], [s, c]])
    t1o = np.arange(h1) + h1 // 2
    k = t1o[:, None, None, None] * f1[None, None, :, None] * n2 + lo[None, None, None, :] * f1[None, None, :, None]
    c, s = _cs(k, n)
    c = c * eye[None, :, None, :] * mirror_weight[None, None, :, None]
    s = s * eye[None, :, None, :] * mirror_weight[None, None, :, None]
    mc = np.concatenate([c.reshape(h1 * g, nf * g), -s.reshape(h1 * g, nf * g)], axis=1)
    return (jnp.asarray(ma, BF16), jnp.asarray(fb, BF16), jnp.asarray(fbi, BF16), jnp.asarray(mc, BF16),
            jnp.asarray(twc, F32), jnp.asarray(tws, F32))


@functools.lru_cache(maxsize=None)
def _one_stage_tables(seq):
    n = 2 * seq
    f = np.arange(n)
    t = np.arange(seq)
    c, s = _cs(f[:, None] * t[None, :], n)
    fwd = np.concatenate([c, -s], axis=0)
    to = np.arange(seq) + seq // 2
    c, s = _cs(to[:, None] * f[None, :], n)
    inv = np.concatenate([c, -s], axis=1)
    return jnp.asarray(fwd, BF16), jnp.asarray(inv, BF16)


@functools.lru_cache(maxsize=None)
def _filter_constants(seq):
    pos = np.arange(seq, dtype=np.float32)
    t = pos / np.float32(max(seq - 1, 1))
    bands = (FILT_EMB - 1) // 2
    fb = np.linspace(1e-4, bands - 1, bands, dtype=np.float32)
    ang = np.float32(2.0 * math.pi / seq) * pos[:, None] * fb[None, :]
    feats = np.zeros((seq, FILT_HID), np.float32)
    feats[:, 0] = t
    feats[:, 1:1 + bands] = np.cos(ang)
    feats[:, 1 + bands:FILT_EMB] = -np.sin(ang)
    dist = np.abs(pos - (seq // 2)) / np.float32(seq / 2.0)
    dist = np.broadcast_to(dist[:, None], (seq, LANES)).astype(np.float32)
    deltas = np.abs(np.linspace(math.log(HY_TARGET) / HY_SLOW_DECAY, math.log(HY_TARGET) / HY_FAST_DECAY,
                                HY_C, dtype=np.float32))
    deltas = np.tile(deltas, HY_ORDER)[None, :]
    return jnp.asarray(feats), jnp.asarray(dist), jnp.asarray(deltas)


def _filter_hidden_body(feats_ref, w1_ref, b1_ref, w2_ref, b2_ref, fr_ref, o_ref):
    fr = fr_ref[...]
    h = jnp.sin(fr * (_dot_exact(feats_ref[...], w1_ref[...]) + b1_ref[...]))
    o_ref[...] = jnp.sin(fr * (_dot_exact(h, w2_ref[...]) + b2_ref[...]))


def _filter_hidden(seq, fweights):
    w1, b1, w2, b2, _, freq = fweights
    depth = w1.shape[0]
    feats, _, _ = _filter_constants(seq)
    mat = pl.BlockSpec((None, FILT_HID, FILT_HID), lambda l: (l, 0, 0))
    vec = pl.BlockSpec((None, 1, FILT_HID), lambda l: (l, 0, 0))
    return pl.pallas_call(
        _filter_hidden_body,
        grid=(depth,),
        in_specs=[pl.BlockSpec((seq, FILT_HID), lambda l: (0, 0)), mat, vec, mat, vec, vec],
        out_specs=pl.BlockSpec((None, seq, FILT_HID), lambda l: (l, 0, 0)),
        out_shape=jax.ShapeDtypeStruct((depth, seq, FILT_HID), F32),
        compiler_params=_params("arbitrary"),
        name="hyena_filter_hidden",
    )(feats, w1, b1, w2, b2, freq)


def _implicit_filter(hid_ref, dist_ref, delt_ref, w3_ref):
    return _dot_exact(hid_ref[...], w3_ref[...]) * jnp.exp(-dist_ref[...] * delt_ref[...])


def _emit_pipelined(n, *stages):
    carried = {}
    for step in range(n + len(stages) - 1):
        for k, stage in enumerate(stages):
            i = step - k
            if 0 <= i < n:
                carried[i] = stage(i, carried.get(i))
    return carried


def _forward_two_stage(src_ref, a_ref, ma_ref, fb_ref, twc_ref, tws_ref, n1, *spectrum_stages):
    n2 = DFT_INNER
    g = ROW_GROUP
    h1 = n1 // 2
    nf = twc_ref.shape[1]
    cw = src_ref.shape[-1]
    half = nf * g

    def transform(i, _):
        xs = [src_ref[:, hi * g:(hi + 1) * g, :].reshape(h1 * g, cw).astype(BF16)
              for hi in range(i * LANE_PAIR, (i + 1) * LANE_PAIR)]
        return _dot(ma_ref[...], jnp.concatenate(xs, axis=1))

    def twiddle(i, out):
        for j in range(LANE_PAIR):
            hi = i * LANE_PAIR + j
            lanes = slice(j * cw, (j + 1) * cw)
            twc = twc_ref[hi]
            tws = tws_ref[hi]
            for f1 in range(nf):
                c = twc[f1:f1 + 1]
                s = tws[f1:f1 + 1]
                re = out[f1 * g:(f1 + 1) * g, lanes]
                im = out[half + f1 * g:half + (f1 + 1) * g, lanes]
                a_ref[hi, f1 * g:(f1 + 1) * g, :] = re * c - im * s
                a_ref[hi, half + f1 * g:half + (f1 + 1) * g, :] = re * s + im * c

    _emit_pipelined(n2 // g // LANE_PAIR, transform, twiddle)

    def spectrum(i, _):
        zs = []
        for f1 in range(i * LANE_PAIR, (i + 1) * LANE_PAIR):
            ar = a_ref[:, f1 * g:(f1 + 1) * g, :].reshape(n2, cw)
            ai = a_ref[:, half + f1 * g:half + (f1 + 1) * g, :].reshape(n2, cw)
            zs.append(jnp.concatenate([ar, ai], axis=0).astype(BF16))
        return _dot(fb_ref[...], jnp.concatenate(zs, axis=1))

    _emit_pipelined(nf // LANE_PAIR, spectrum, *spectrum_stages)


def _filter_two_stage_body(hid_ref, dist_ref, delt_ref, w3_ref, ma_ref, fb_ref, twc_ref, tws_ref,
                           o_ref, h_ref, a_ref, *, n1):
    h = _implicit_filter(hid_ref, dist_ref, delt_ref, w3_ref)
    h_ref[...] = h.reshape(h_ref.shape)
    inv_n = 1.0 / (n1 * DFT_INNER)
    cw = h_ref.shape[-1]

    def store(i, spec):
        spec = (spec * inv_n).astype(o_ref.dtype)
        for j in range(LANE_PAIR):
            o_ref[i * LANE_PAIR + j] = spec[:, j * cw:(j + 1) * cw]

    _forward_two_stage(h_ref, a_ref, ma_ref, fb_ref, twc_ref, tws_ref, n1, store)


def _filter_one_stage_body(hid_ref, dist_ref, delt_ref, w3_ref, fwd_ref, o_ref):
    h = _implicit_filter(hid_ref, dist_ref, delt_ref, w3_ref)
    inv_n = 1.0 / fwd_ref.shape[0] * 2.0
    o_ref[...] = (_dot(fwd_ref[...], h.astype(BF16)) * inv_n).astype(o_ref.dtype)


def _filter_input_specs(seq, cw):
    return [
        pl.BlockSpec((None, seq, FILT_HID), lambda l, j: (l, 0, 0)),
        pl.BlockSpec((seq, LANES), lambda l, j: (0, 0)),
        pl.BlockSpec((1, cw), lambda l, j: (0, j)),
        pl.BlockSpec((None, FILT_HID, cw), lambda l, j: (l, 0, j)),
    ]


def _filter_weights(hy_f_w1, hy_f_b1, hy_f_w2, hy_f_b2, hy_f_w3, hy_f_freq):
    depth = hy_f_w1.shape[0]
    w1 = jnp.pad(hy_f_w1, ((0, 0), (0, FILT_HID - FILT_EMB), (0, 0)))
    return (w1, hy_f_b1.reshape(depth, 1, FILT_HID), hy_f_w2, hy_f_b2.reshape(depth, 1, FILT_HID),
            hy_f_w3, hy_f_freq.reshape(depth, 1, FILT_HID))


def _filter_spectra_two_stage(seq, fweights):
    depth = fweights[0].shape[0]
    cw = LANES
    n2 = DFT_INNER
    n1 = 2 * seq // n2
    _, dist, deltas = _filter_constants(seq)
    ma, fb, _, _, twc, tws = _two_stage_tables(seq)
    nf = twc.shape[1]
    ncol = HY_ORDER * HY_C
    return pl.pallas_call(
        functools.partial(_filter_two_stage_body, n1=n1),
        grid=(depth, ncol // cw),
        in_specs=[
            *_filter_input_specs(seq, cw),
            pl.BlockSpec(ma.shape, lambda l, j: (0, 0)),
            pl.BlockSpec(fb.shape, lambda l, j: (0, 0)),
            pl.BlockSpec(twc.shape, lambda l, j: (0, 0, 0)),
            pl.BlockSpec(tws.shape, lambda l, j: (0, 0, 0)),
        ],
        out_specs=pl.BlockSpec((None, nf, 2 * n2, cw), lambda l, j: (l, 0, 0, j)),
        out_shape=jax.ShapeDtypeStruct((depth, nf, 2 * n2, ncol), BF16),
        scratch_shapes=[pltpu.VMEM((n1 // 2, n2, cw), F32),
                        pltpu.VMEM((n2 // ROW_GROUP, 2 * nf * ROW_GROUP, cw), F32)],
        compiler_params=_params("arbitrary", "arbitrary"),
        name="hyena_filter_long",
    )(_filter_hidden(seq, fweights), dist, deltas, fweights[4], ma, fb, twc, tws)


def _filter_spectra_one_stage(seq, fweights):
    depth = fweights[0].shape[0]
    cw = LANES
    _, dist, deltas = _filter_constants(seq)
    fwd, _ = _one_stage_tables(seq)
    ncol = HY_ORDER * HY_C
    return pl.pallas_call(
        _filter_one_stage_body,
        grid=(depth, ncol // cw),
        in_specs=[*_filter_input_specs(seq, cw), pl.BlockSpec(fwd.shape, lambda l, j: (0, 0))],
        out_specs=pl.BlockSpec((None, 4 * seq, cw), lambda l, j: (l, 0, j)),
        out_shape=jax.ShapeDtypeStruct((depth, 4 * seq, ncol), BF16),
        compiler_params=_params("arbitrary", "arbitrary"),
        name="hyena_filter_short",
    )(_filter_hidden(seq, fweights), dist, deltas, fweights[4], fwd)


def _short_conv(x, w_ref, b_ref, part):
    seq = x.shape[0]
    rows = lax.broadcasted_iota(jnp.int32, x.shape, 0)
    prev = jnp.where(rows == 0, 0.0, pltpu.roll(x, 1, 0))
    nxt = jnp.where(rows == seq - 1, 0.0, pltpu.roll(x, seq - 1, 0))
    w = w_ref[part]
    return prev * w[0:1] + x * w[1:2] + nxt * w[2:3] + b_ref[part:part + 1, :]


def _complex_mul(x, h, half):
    xr, xi = x[:half], x[half:]
    hr, hi = h[:half].astype(F32), h[half:].astype(F32)
    return jnp.concatenate([xr * hr - xi * hi, xr * hi + xi * hr], axis=0)


def _hyena_long_body(v_ref, x1_ref, x2_ref, g_ref, cw_ref, cb_ref, skip_ref, hf0_ref, hf1_ref,
                     ma_ref, fb_ref, fbi_ref, mc_ref, twc_ref, tws_ref, o_ref, y_ref, c_ref, a_ref, w_ref,
                     *, n1):
    n2 = DFT_INNER
    g = ROW_GROUP
    h1 = n1 // 2
    seq, cw = v_ref.shape
    nf = twc_ref.shape[1]
    half = nf * g

    def long_conv(hf_ref):
        def filter_and_invert(i, spec):
            hf = jnp.concatenate([hf_ref[i * LANE_PAIR + j] for j in range(LANE_PAIR)], axis=1)
            return _dot(fbi_ref[...], _complex_mul(spec, hf, n2).astype(BF16))

        def store_inverse(i, w):
            for j in range(LANE_PAIR):
                r0 = (i * LANE_PAIR + j) * g
                r1 = half + r0
                w_ref[:, r0:r0 + g, :] = w[:n2, j * cw:(j + 1) * cw].reshape(n2 // g, g, cw)
                w_ref[:, r1:r1 + g, :] = w[n2:, j * cw:(j + 1) * cw].reshape(n2 // g, g, cw)

        _forward_two_stage(y_ref, a_ref, ma_ref, fb_ref, twc_ref, tws_ref, n1, filter_and_invert, store_inverse)

        def untwiddle(i, _):
            cols = []
            for hi in range(i * LANE_PAIR, (i + 1) * LANE_PAIR):
                res, ims = [], []
                twc = twc_ref[hi]
                tws = tws_ref[hi]
                for f1 in range(nf):
                    c = twc[f1:f1 + 1]
                    s = tws[f1:f1 + 1]
                    re = w_ref[hi, f1 * g:(f1 + 1) * g, :]
                    im = w_ref[hi, half + f1 * g:half + (f1 + 1) * g, :]
                    res.append(re * c + im * s)
                    ims.append(im * c - re * s)
                cols.append(jnp.concatenate(res + ims, axis=0).astype(BF16))
            return jnp.concatenate(cols, axis=1)

        def to_time(i, wc):
            y = _dot(mc_ref[...], wc)
            for j in range(LANE_PAIR):
                hi = i * LANE_PAIR + j
                c_ref[:, hi * g:(hi + 1) * g, :] = y[:, j * cw:(j + 1) * cw].reshape(h1, g, cw)

        _emit_pipelined(n2 // g // LANE_PAIR, untwiddle, to_time)

    def load3(ref):
        return ref[...].reshape(seq, cw)

    y_ref[...] = _short_conv(v_ref[...], cw_ref, cb_ref, 0).reshape(y_ref.shape)
    long_conv(hf0_ref)
    y = _short_conv(x1_ref[...], cw_ref, cb_ref, 1) * (load3(c_ref) + skip_ref[0:1, :] * load3(y_ref))
    y_ref[...] = y.reshape(y_ref.shape)
    long_conv(hf1_ref)
    out = _short_conv(x2_ref[...], cw_ref, cb_ref, 2) * (load3(c_ref) + skip_ref[1:2, :] * load3(y_ref))
    o_ref[...] = (out * _silu(g_ref[...])).astype(o_ref.dtype)


def _const_spec(shape):
    nd = len(shape)
    return pl.BlockSpec(shape, lambda *_: (0,) * nd, pipeline_mode=pl.Buffered(1))


def _hyena_long(p, conv_w, conv_b, skip, spectra, layer):
    b, seq, _ = p.shape
    cw = LANES
    n2 = DFT_INNER
    n1 = 2 * seq // n2
    nblk = HY_C // cw
    ma, fb, fbi, mc, twc, tws = _two_stage_tables(seq)
    nf = twc.shape[1]

    def pcol(off):
        return pl.BlockSpec((None, seq, cw), lambda j, i: (i, 0, off // cw + j))

    def hf_spec(order):
        return pl.BlockSpec((None, nf, 2 * n2, cw), lambda j, i: (layer, 0, 0, order * nblk + j),
                            pipeline_mode=pl.Buffered(1))

    stage_rows = (n2 // ROW_GROUP, 2 * nf * ROW_GROUP, cw)
    return pl.pallas_call(
        functools.partial(_hyena_long_body, n1=n1),
        grid=(nblk, b),
        in_specs=[
            pcol(OFF_HYZ), pcol(OFF_HYZ + HY_C), pcol(OFF_HYZ + 2 * HY_C), pcol(OFF_HYG),
            pl.BlockSpec((None, HY_ORDER + 1, SHORT_K, cw), lambda j, i: (layer, 0, 0, j)),
            pl.BlockSpec((None, HY_ORDER + 1, cw), lambda j, i: (layer, 0, j)),
            pl.BlockSpec((None, HY_ORDER, cw), lambda j, i: (layer, 0, j)),
            hf_spec(0), hf_spec(1),
            _const_spec(ma.shape), _const_spec(fb.shape), _const_spec(fbi.shape), _const_spec(mc.shape),
            _const_spec(twc.shape), _const_spec(tws.shape),
        ],
        out_specs=pl.BlockSpec((None, seq, cw), lambda j, i: (i, 0, j)),
        out_shape=jax.ShapeDtypeStruct((b, seq, HY_C), BF16),
        scratch_shapes=[
            pltpu.VMEM((n1 // 2, n2, cw), F32), pltpu.VMEM((n1 // 2, n2, cw), F32),
            pltpu.VMEM(stage_rows, F32), pltpu.VMEM(stage_rows, F32),
        ],
        compiler_params=_params("arbitrary", "arbitrary"),
        name="hyena_long",
    )(p, p, p, p, conv_w, conv_b, skip, spectra, spectra, ma, fb, fbi, mc, twc, tws)


def _hyena_short_body(v_ref, x1_ref, x2_ref, g_ref, cw_ref, cb_ref, skip_ref, hf0_ref, hf1_ref,
                      fwd_ref, inv_ref, o_ref):
    n = fwd_ref.shape[0] // 2

    def long_conv(y, hf_ref):
        spec = _complex_mul(_dot(fwd_ref[...], y.astype(BF16)), hf_ref[...], n)
        return _dot(inv_ref[...], spec.astype(BF16))

    y = _short_conv(v_ref[...], cw_ref, cb_ref, 0)
    y = _short_conv(x1_ref[...], cw_ref, cb_ref, 1) * (long_conv(y, hf0_ref) + skip_ref[0:1, :] * y)
    y = _short_conv(x2_ref[...], cw_ref, cb_ref, 2) * (long_conv(y, hf1_ref) + skip_ref[1:2, :] * y)
    o_ref[...] = (y * _silu(g_ref[...])).astype(o_ref.dtype)


def _hyena_short(p, conv_w, conv_b, skip, spectra, layer):
    b, seq, _ = p.shape
    cw = LANE_PAIR * LANES
    nblk = HY_C // cw
    fwd, inv = _one_stage_tables(seq)

    def pcol(off):
        return pl.BlockSpec((None, seq, cw), lambda j, i: (i, 0, off // cw + j))

    def hf_spec(order):
        return pl.BlockSpec((None, 4 * seq, cw), lambda j, i: (layer, 0, order * nblk + j))

    return pl.pallas_call(
        _hyena_short_body,
        grid=(nblk, b),
        in_specs=[
            pcol(OFF_HYZ), pcol(OFF_HYZ + HY_C), pcol(OFF_HYZ + 2 * HY_C), pcol(OFF_HYG),
            pl.BlockSpec((None, HY_ORDER + 1, SHORT_K, cw), lambda j, i: (layer, 0, 0, j)),
            pl.BlockSpec((None, HY_ORDER + 1, cw), lambda j, i: (layer, 0, j)),
            pl.BlockSpec((None, HY_ORDER, cw), lambda j, i: (layer, 0, j)),
            hf_spec(0), hf_spec(1),
            pl.BlockSpec(fwd.shape, lambda j, i: (0, 0)), pl.BlockSpec(inv.shape, lambda j, i: (0, 0)),
        ],
        out_specs=pl.BlockSpec((None, seq, cw), lambda j, i: (i, 0, j)),
        out_shape=jax.ShapeDtypeStruct((b, seq, HY_C), BF16),
        compiler_params=_params("arbitrary", "arbitrary"),
        name="hyena_short",
    )(p, p, p, p, conv_w, conv_b, skip, spectra, spectra, fwd, inv)


def _chunk_mlp_body(u_ref, v_ref, g_ref, lg_ref, lb_ref, ws_ref, bs_ref, o_ref):
    v = v_ref[...]
    mu = jnp.mean(v, axis=-1, keepdims=True)
    vc = v - mu
    var = jnp.mean(vc * vc, axis=-1, keepdims=True)
    vn = (vc * lax.rsqrt(var + EPS) * lg_ref[...] + lb_ref[...]).astype(BF16)
    rows = v.shape[0]
    for n in range(rows // CHUNK):
        r = slice(n * CHUNK, (n + 1) * CHUNK)
        for h in range(CM_HEADS):
            c = slice(h * CM_HD, (h + 1) * CM_HD)
            s = _dot(ws_ref[h].astype(BF16), vn[r, c]) + bs_ref[:, c]
            o_ref[r, c] = (u_ref[r, c] * s * _silu(g_ref[r, c])).astype(o_ref.dtype)


def _chunk_mlp(p, ln_g, ln_b, w_s, b_s, layer):
    b, seq, _ = p.shape
    bm = min(seq, 512)
    bs_full = jnp.repeat(jnp.swapaxes(b_s, 1, 2), CM_HD, axis=2)

    def pcol(off):
        return pl.BlockSpec((None, bm, CM_C), lambda i, t: (i, t, off // CM_C))

    return pl.pallas_call(
        _chunk_mlp_body,
        grid=(b, seq // bm),
        in_specs=[
            pcol(OFF_CMU), pcol(OFF_CMV), pcol(OFF_CMG),
            pl.BlockSpec((None, 1, CM_C), lambda i, t: (layer, 0, 0)),
            pl.BlockSpec((None, 1, CM_C), lambda i, t: (layer, 0, 0)),
            pl.BlockSpec((None, CM_HEADS, CHUNK, CHUNK), lambda i, t: (layer, 0, 0, 0)),
            pl.BlockSpec((None, CHUNK, CM_C), lambda i, t: (layer, 0, 0)),
        ],
        out_specs=pl.BlockSpec((None, bm, CM_C), lambda i, t: (i, t, 0)),
        out_shape=jax.ShapeDtypeStruct((b, seq, CM_C), BF16),
        compiler_params=_params("arbitrary", "arbitrary"),
        name="chunk_mlp",
    )(p, p, p, ln_g.reshape(-1, 1, CM_C), ln_b.reshape(-1, 1, CM_C), w_s, bs_full)


def _rms_heads(x, g, nheads):
    outs = []
    for h in range(nheads):
        xh = x[:, h * HEAD_DIM:(h + 1) * HEAD_DIM]
        ms = jnp.mean(xh * xh, axis=-1, keepdims=True)
        outs.append(xh * lax.rsqrt(ms + EPS) * g)
    return outs


def _rope(xh, cos, sin):
    lane = lax.broadcasted_iota(jnp.int32, xh.shape, 1)
    quarter = ROPE_AXIS // 2
    partner = jnp.where((lane % ROPE_AXIS) < quarter,
                        pltpu.roll(xh, HEAD_DIM - quarter, 1), pltpu.roll(xh, quarter, 1))
    return xh * cos + partner * sin


def _kv_prep_body(k_ref, v_ref, gk_ref, cos_ref, sin_ref, kn_ref, kr_ref, vt_ref, *, rotary):
    heads = _rms_heads(k_ref[...], gk_ref[...], N_KV_HEADS)
    for h, kh in enumerate(heads):
        c = slice(h * HEAD_DIM, (h + 1) * HEAD_DIM)
        kn_ref[:, c] = kh
        if rotary:
            kh = _rope(kh, cos_ref[...], sin_ref[...])
        kr_ref[:, c] = kh.astype(BF16)
        vt_ref[h, 0:HEAD_DIM, :] = v_ref[:, c].T.astype(BF16)
        vt_ref[h, HEAD_DIM:, :] = jnp.ones((ATT_ONES_ROWS, v_ref.shape[0]), BF16)


def _kv_prep(p, k_norm_g, cos, sin, layer, rotary):
    b, seq, _ = p.shape
    bm = min(seq, 512)
    blk = pl.BlockSpec((None, bm, KV_C), lambda i, t: (i, t, 0))
    vt_blk = pl.BlockSpec((None, N_KV_HEADS, HEAD_DIM + ATT_ONES_ROWS, bm), lambda i, t: (i, 0, 0, t))
    return pl.pallas_call(
        functools.partial(_kv_prep_body, rotary=rotary),
        grid=(b, seq // bm),
        in_specs=[
            pl.BlockSpec((None, bm, KV_C), lambda i, t: (i, t, OFF_K // KV_C)),
            pl.BlockSpec((None, bm, KV_C), lambda i, t: (i, t, OFF_V // KV_C)),
            pl.BlockSpec((None, 1, HEAD_DIM), lambda i, t: (layer, 0, 0)),
            pl.BlockSpec((bm, HEAD_DIM), lambda i, t: (t, 0)),
            pl.BlockSpec((bm, HEAD_DIM), lambda i, t: (t, 0)),
        ],
        out_specs=[blk, blk, vt_blk],
        out_shape=[jax.ShapeDtypeStruct((b, seq, KV_C), F32),
                   jax.ShapeDtypeStruct((b, seq, KV_C), BF16),
                   jax.ShapeDtypeStruct((b, N_KV_HEADS, HEAD_DIM + ATT_ONES_ROWS, seq), BF16)],
        compiler_params=_params("arbitrary", "arbitrary"),
        name="kv_prep",
    )(p, p, k_norm_g.reshape(-1, 1, HEAD_DIM), cos, sin)


def _attention_body(*refs, rotary, has_ctx):
    q_ref, g_ref, gq_ref, cos_ref, sin_ref, k_ref, vt_ref = refs[:7]
    if has_ctx:
        ck_ref, cv_ref, o_ref, s_ref, p_ref = refs[7:]
    else:
        o_ref, s_ref, p_ref = refs[7:]
    nslot, s_all, nq = s_ref.shape
    tq = nq // GQA
    nsub = q_ref.shape[0] // tq
    s_self = k_ref.shape[0]
    chunk = min(ATT_KEY_CHUNK, s_self)
    sub = ROW_GROUP
    scale = math.log2(math.e) / math.sqrt(HEAD_DIM)
    contract_last = (((1,), (1,)), ((), ()))
    key_blocks = [slice(j * chunk, (j + 1) * chunk) for j in range(s_self // chunk)]
    if has_ctx:
        ck = ck_ref[...].astype(BF16)
        cvt = jnp.concatenate([cv_ref[...].T.astype(BF16),
                               jnp.ones((ATT_ONES_ROWS, cv_ref.shape[0]), BF16)], axis=0)

    def col_reduce(x, op):
        return op(x.reshape(x.shape[0] // sub, sub, nq), axis=0)

    blocks = key_blocks + ([slice(s_self, s_all)] if has_ctx else [])

    def scores(t):
        rows_q = slice(t * tq, (t + 1) * tq)
        heads = _rms_heads(q_ref[rows_q, :], gq_ref[...], GQA)
        if rotary:
            heads = [_rope(qh, cos_ref[rows_q, :], sin_ref[rows_q, :]) for qh in heads]
        q = jnp.concatenate([(qh * scale).astype(BF16) for qh in heads], axis=0)
        slot = t % nslot
        m8 = None
        for i, rows in enumerate(blocks):
            kb = ck if (has_ctx and i == len(blocks) - 1) else k_ref[rows, :]
            st = lax.dot_general(kb, q, contract_last, preferred_element_type=F32)
            s_ref[slot, rows, :] = st
            part = col_reduce(st, jnp.max)
            m8 = part if m8 is None else jnp.maximum(m8, part)
            yield None if i < len(blocks) - 1 else jnp.max(m8, axis=0, keepdims=True)

    def weights(t, m):
        slot = t % nslot
        for rows in blocks:
            p_ref[slot, rows, :] = jnp.exp2(s_ref[slot, rows, :] - m).astype(BF16)
            yield None

    def values(t):
        slot = t % nslot
        rows_q = slice(t * tq, (t + 1) * tq)
        ot = None
        for i, rows in enumerate(blocks):
            vtb = cvt if (has_ctx and i == len(blocks) - 1) else vt_ref[:, rows]
            part = _dot(vtb, p_ref[slot, rows, :])
            ot = part if ot is None else ot + part
            if i == len(blocks) - 1:
                ot = ot[0:HEAD_DIM] / ot[HEAD_DIM:HEAD_DIM + 1]
                for h in range(GQA):
                    c = slice(h * HEAD_DIM, (h + 1) * HEAD_DIM)
                    oh = ot[:, h * tq:(h + 1) * tq].T
                    o_ref[rows_q, c] = (oh * _silu(g_ref[rows_q, c])).astype(o_ref.dtype)
            yield None

    col_max = {}
    for phase in range(nsub + 2):
        stages = []
        if phase < nsub:
            stages.append(("scores", scores(phase)))
        if 0 <= phase - 1 < nsub:
            stages.append(("weights", weights(phase - 1, col_max[phase - 1])))
        if 0 <= phase - 2 < nsub:
            stages.append(("values", values(phase - 2)))
        for _ in blocks:
            for name, steps in stages:
                out = next(steps)
                if name == "scores":
                    col_max[phase] = out


def _attention(p, q_norm_g, cos, sin, keys, values_t, ctx_k, ctx_v, layer, rotary, tq):
    b, seq, _ = p.shape
    skv = keys.shape[1]
    gw = GQA * HEAD_DIM
    has_ctx = ctx_k is not None
    in_specs = [
        pl.BlockSpec((None, tq, gw), lambda i, h, t: (i, t, OFF_Q // gw + h)),
        pl.BlockSpec((None, tq, gw), lambda i, h, t: (i, t, OFF_ATG // gw + h)),
        pl.BlockSpec((None, 1, HEAD_DIM), lambda i, h, t: (layer, 0, 0)),
        pl.BlockSpec((tq, HEAD_DIM), lambda i, h, t: (t, 0)),
        pl.BlockSpec((tq, HEAD_DIM), lambda i, h, t: (t, 0)),
        pl.BlockSpec((None, skv, HEAD_DIM), lambda i, h, t: (i, 0, h)),
        pl.BlockSpec((None, None, HEAD_DIM + ATT_ONES_ROWS, skv), lambda i, h, t: (i, h, 0, 0)),
    ]
    args = [p, p, q_norm_g.reshape(-1, 1, HEAD_DIM), cos, sin, keys, values_t]
    s_all = skv
    if has_ctx:
        past = ctx_k.shape[2]
        spec = pl.BlockSpec((None, None, past, HEAD_DIM), lambda i, h, t: (i, layer, 0, h))
        in_specs += [spec, spec]
        args += [ctx_k, ctx_v]
        s_all += past
    slots = (ATT_PIPE_SLOTS, s_all, GQA * ATT_SUBTILE_ROWS)
    return pl.pallas_call(
        functools.partial(_attention_body, rotary=rotary, has_ctx=has_ctx),
        grid=(b, N_KV_HEADS, seq // tq),
        in_specs=in_specs,
        out_specs=pl.BlockSpec((None, tq, gw), lambda i, h, t: (i, t, h)),
        out_shape=jax.ShapeDtypeStruct((b, seq, ATT_C), BF16),
        scratch_shapes=[pltpu.VMEM(slots, F32), pltpu.VMEM(slots, BF16)],
        compiler_params=_params("arbitrary", "arbitrary", "arbitrary"),
        name="attention",
    )(*args)


def _out_proj_body(yh_ref, yc_ref, ya_ref, x_ref, mod_ref, w_ref, lg_ref, lb_ref, o_ref):
    d = x_ref.shape[-1]
    gate = mod_ref[:, 2 * d:3 * d]
    mixed = jnp.concatenate([yh_ref[...], yc_ref[...], ya_ref[...]], axis=1)
    r = DN_ALPHA * x_ref[...] + gate * _dot(mixed, w_ref[...])
    mu = jnp.mean(r, axis=-1, keepdims=True)
    rc = r - mu
    var = jnp.mean(rc * rc, axis=-1, keepdims=True)
    o_ref[...] = rc * lax.rsqrt(var + EPS) * lg_ref[...] + lb_ref[...]


def _out_proj(y_hy, y_cm, y_at, x, mod, row0, w_out_bf, ln_g, ln_b, layer):
    g, t, d = x.shape
    bm = 512

    def rows(c):
        return pl.BlockSpec((None, bm, c), lambda b, i: (b, i, 0))

    return pl.pallas_call(
        _out_proj_body,
        grid=(g, t // bm),
        in_specs=[
            rows(HY_C), rows(CM_C), rows(ATT_C), rows(d),
            pl.BlockSpec((None, None, 1, 3 * d), lambda b, i: (layer, row0 + b, 0, 0)),
            pl.BlockSpec((None, D_MIX, d), lambda b, i: (layer, 0, 0)),
            pl.BlockSpec((None, 1, d), lambda b, i: (layer, 0, 0)),
            pl.BlockSpec((None, 1, d), lambda b, i: (layer, 0, 0)),
        ],
        out_specs=rows(d),
        out_shape=jax.ShapeDtypeStruct((g, t, d), F32),
        compiler_params=_params("arbitrary", "arbitrary"),
        name="out_proj_ln",
    )(y_hy, y_cm, y_at, x, mod, w_out_bf, ln_g.reshape(-1, 1, d), ln_b.reshape(-1, 1, d))


@functools.lru_cache(maxsize=None)
def _rope_tables(seq):
    rows = seq // GRID_W
    row = np.repeat(np.arange(rows, dtype=np.float32), GRID_W)
    col = np.tile(np.arange(GRID_W, dtype=np.float32), rows)
    inv = (np.float32(ROPE_THETA) ** (-np.arange(0, ROPE_AXIS, 2, dtype=np.float32) / np.float32(ROPE_AXIS))
           ).astype(np.float32)
    ra = row[:, None] * inv[None, :]
    ca = col[:, None] * inv[None, :]
    cos = np.concatenate([np.cos(ra), np.cos(ra), np.cos(ca), np.cos(ca)], axis=1)
    sin = np.concatenate([-np.sin(ra), np.sin(ra), -np.sin(ca), np.sin(ca)], axis=1)
    return jnp.asarray(cos, F32), jnp.asarray(sin, F32)


def _mixer_layer(x, seq, mod, row0, layer, rotary, w_in_bf, w_out_bf, spectra, hy_conv_w, hy_conv_b, hy_skip,
                 cm_ln_g, cm_ln_b, cm_w_s, cm_b_s, q_norm_g, k_norm_g, ln_g, ln_b, ctx_k, ctx_v, cos, sin):
    g, t, d = x.shape
    nreq = g * t // seq
    p = _in_proj(x, mod, row0, w_in_bf, layer).reshape(nreq, seq, D_IN)
    if rotary:
        y_hy = _hyena_long(p, hy_conv_w, hy_conv_b, hy_skip, spectra, layer)
    else:
        y_hy = _hyena_short(p, hy_conv_w, hy_conv_b, hy_skip, spectra, layer)
    y_cm = _chunk_mlp(p, cm_ln_g, cm_ln_b, cm_w_s, cm_b_s, layer)
    k_normed, keys, values = _kv_prep(p, k_norm_g, cos, sin, layer, rotary)
    y_at = _attention(p, q_norm_g, cos, sin, keys, values, ctx_k, ctx_v, layer, rotary,
                      tq=min(seq, ATT_SUBTILES * ATT_SUBTILE_ROWS))
    x_new = _out_proj(y_hy.reshape(g, t, HY_C), y_cm.reshape(g, t, CM_C), y_at.reshape(g, t, ATT_C),
                      x, mod, row0, w_out_bf, ln_g, ln_b, layer)
    return x_new, k_normed, p


def kernel(x_prompt, x_sample, cache_k, cache_v, c, c_ctx, w_mod, b_mod, w_in, hy_conv_w, hy_conv_b, hy_f_w1, hy_f_b1, hy_f_w2, hy_f_b2, hy_f_w3, hy_f_freq, hy_skip, cm_ln_g, cm_ln_b, cm_w_s, cm_b_s, q_norm_g, k_norm_g, w_out, ln_g, ln_b):
    batch, seq, d = x_prompt.shape
    dec_batch, dec_seq, _ = x_sample.shape
    depth = w_in.shape[0]
    past = cache_k.shape[2]

    cond = jnp.concatenate([c_ctx[None, :], c, jnp.zeros((MOD_ROWS - 1 - dec_batch, d), F32)], axis=0)
    mod = _modulation(cond, w_mod, b_mod).reshape(depth, MOD_ROWS, 1, 3 * d)

    w_in_bf = w_in.astype(BF16)
    w_out_bf = w_out.astype(BF16)
    fweights = _filter_weights(hy_f_w1, hy_f_b1, hy_f_w2, hy_f_b2, hy_f_w3, hy_f_freq)
    spectra_ctx = _filter_spectra_one_stage(seq, fweights)
    spectra_lat = _filter_spectra_two_stage(dec_seq, fweights)
    conv_w = hy_conv_w.reshape(depth, SHORT_K, HY_ORDER + 1, HY_C).transpose(0, 2, 1, 3)
    conv_b = hy_conv_b.reshape(depth, HY_ORDER + 1, HY_C)
    cos, sin = _rope_tables(dec_seq)
    ctx_k = cache_k.reshape(dec_batch, depth, past, KV_C)
    ctx_v = cache_v.reshape(dec_batch, depth, past, KV_C)

    xp = x_prompt.reshape(1, batch * seq, d)
    xs = x_sample
    new_k, new_v = [], []
    for layer in range(depth):
        common = dict(layer=layer, w_in_bf=w_in_bf, w_out_bf=w_out_bf, hy_conv_w=conv_w, hy_conv_b=conv_b,
                      hy_skip=hy_skip, cm_ln_g=cm_ln_g, cm_ln_b=cm_ln_b, cm_w_s=cm_w_s, cm_b_s=cm_b_s,
                      q_norm_g=q_norm_g, k_norm_g=k_norm_g, ln_g=ln_g, ln_b=ln_b, cos=cos, sin=sin)
        xp, k_l, p_ctx = _mixer_layer(xp, seq, mod, 0, rotary=False, spectra=spectra_ctx,
                                      ctx_k=None, ctx_v=None, **common)
        new_k.append(k_l.reshape(batch, seq, N_KV_HEADS, HEAD_DIM))
        new_v.append(p_ctx[:, :, OFF_V:OFF_V + KV_C].reshape(batch, seq, N_KV_HEADS, HEAD_DIM))
        xs, _, _ = _mixer_layer(xs, dec_seq, mod, 1, rotary=True, spectra=spectra_lat,
                                ctx_k=ctx_k, ctx_v=ctx_v, **common)
    return (xp.reshape(batch, seq, d), xs, jnp.stack(new_k, axis=1), jnp.stack(new_v, axis=1))
```

```python
import functools
import math

import numpy as np
import jax
import jax.numpy as jnp
from jax import lax
from jax.experimental import pallas as pl
from jax.experimental.pallas import tpu as pltpu

F32 = jnp.float32
BF16 = jnp.bfloat16

D_MODEL = 2048
DEPTH = 4
GRID_W = 64
HY_C = 512
HY_ORDER = 2
SHORT_K = 3
FILT_EMB = 33
FILT_HID = 64
HY_FAST_DECAY = 0.3
HY_SLOW_DECAY = 1.5
HY_TARGET = 1e-2
CM_HEADS = 4
CM_HD = 128
CM_C = CM_HEADS * CM_HD
CHUNK = 128
N_HEADS = 8
N_KV_HEADS = 2
HEAD_DIM = 128
GQA = N_HEADS // N_KV_HEADS
ATT_C = N_HEADS * HEAD_DIM
KV_C = N_KV_HEADS * HEAD_DIM
ROPE_THETA = 10000.0
ROPE_AXIS = HEAD_DIM // 2
D_MIX = HY_C + CM_C + ATT_C
EPS = 1e-6
DN_ALPHA = (2 * DEPTH) ** 0.25

OFF_HYZ = 0
OFF_HYG = (HY_ORDER + 1) * HY_C
OFF_CMU = OFF_HYG + HY_C
OFF_CMV = OFF_CMU + CM_C
OFF_CMG = OFF_CMV + CM_C
OFF_Q = OFF_CMG + CM_C
OFF_K = OFF_Q + ATT_C
OFF_V = OFF_K + KV_C
OFF_ATG = OFF_V + KV_C
D_IN = OFF_ATG + ATT_C

LANES = 128
VMEM_LIMIT_BYTES = 56 * 1024 * 1024

DFT_INNER = 128
ROW_GROUP = 8
LANE_PAIR = 2
ATT_KEY_CHUNK = 512
ATT_SUBTILE_ROWS = 128
ATT_ONES_ROWS = 16
ATT_PIPE_SLOTS = 2
ATT_SUBTILES = 16
MOD_ROWS = 16


def _params(*semantics):
    return pltpu.CompilerParams(dimension_semantics=semantics, vmem_limit_bytes=VMEM_LIMIT_BYTES)


def _silu(x):
    return x * jax.nn.sigmoid(x)


def _dot(a, b):
    return jnp.dot(a, b, preferred_element_type=F32)


def _dot_exact(a, b):
    return jnp.dot(a, b, preferred_element_type=F32, precision=lax.Precision.HIGHEST)


def _mod_body(c_ref, w_ref, b_ref, o_ref):
    s = _silu(c_ref[...]).astype(BF16)
    o_ref[...] = _dot(s, w_ref[...].astype(BF16)) + b_ref[...]


def _modulation(cond, w_mod, b_mod):
    depth, d, d3 = w_mod.shape
    bn = 1536
    return pl.pallas_call(
        _mod_body,
        grid=(depth, d3 // bn),
        in_specs=[
            pl.BlockSpec((MOD_ROWS, d), lambda l, j: (0, 0)),
            pl.BlockSpec((None, d, bn), lambda l, j: (l, 0, j)),
            pl.BlockSpec((None, 1, bn), lambda l, j: (l, 0, j)),
        ],
        out_specs=pl.BlockSpec((None, MOD_ROWS, bn), lambda l, j: (l, 0, j)),
        out_shape=jax.ShapeDtypeStruct((depth, MOD_ROWS, d3), F32),
        compiler_params=_params("arbitrary", "arbitrary"),
        name="modulation",
    )(cond, w_mod, b_mod.reshape(depth, 1, d3))


def _in_proj_body(x_ref, mod_ref, w_ref, o_ref):
    d = x_ref.shape[-1]
    shift = mod_ref[:, 0:d]
    scale = mod_ref[:, d:2 * d]
    h = (x_ref[...] * (1.0 + scale) + shift).astype(BF16)
    o_ref[...] = _dot(h, w_ref[...])


def _in_proj(x, mod, row0, w_in_bf, layer):
    g, t, d = x.shape
    d_in = w_in_bf.shape[-1]
    bm, bn = 1024, 1024
    return pl.pallas_call(
        _in_proj_body,
        grid=(g, t // bm, d_in // bn),
        in_specs=[
            pl.BlockSpec((None, bm, d), lambda b, i, j: (b, i, 0)),
            pl.BlockSpec((None, None, 1, 3 * d), lambda b, i, j: (layer, row0 + b, 0, 0)),
            pl.BlockSpec((None, d, bn), lambda b, i, j: (layer, 0, j)),
        ],
        out_specs=pl.BlockSpec((None, bm, bn), lambda b, i, j: (b, i, j)),
        out_shape=jax.ShapeDtypeStruct((g, t, d_in), F32),
        compiler_params=_params("arbitrary", "arbitrary", "arbitrary"),
        name="in_proj",
    )(x, mod, w_in_bf)


def _cs(k, n):
    ang = 2.0 * np.pi * (np.asarray(k, dtype=np.int64) % n).astype(np.float64) / n
    return np.cos(ang), np.sin(ang)


@functools.lru_cache(maxsize=None)
def _two_stage_tables(seq):
    n = 2 * seq
    n2 = DFT_INNER
    n1 = n // n2
    h1 = n1 // 2
    g = ROW_GROUP
    nf = -(-(n1 // 2 + 1) // LANE_PAIR) * LANE_PAIR
    t1 = np.arange(h1)
    f1 = np.arange(nf)
    live = (f1 <= n1 // 2).astype(np.float64)
    mirror_weight = np.where((f1 == 0) | (f1 == n1 // 2), 1.0, 2.0) * live
    t2 = np.arange(n2)
    f2 = np.arange(n2)
    lo = np.arange(g)
    eye = np.eye(g)
    k = t1[None, None, :, None] * f1[:, None, None, None] * n2 + lo[None, None, None, :] * f1[:, None, None, None]
    c, s = _cs(k, n)
    c = c * eye[None, :, None, :] * live[:, None, None, None]
    s = s * eye[None, :, None, :] * live[:, None, None, None]
    ma = np.concatenate([c.reshape(nf * g, h1 * g), -s.reshape(nf * g, h1 * g)], axis=0)
    hi = np.arange(n2 // g)
    c, s = _cs(g * hi[:, None] * f1[None, :], n)
    twc = np.broadcast_to(c[:, :, None], c.shape + (LANES,))
    tws = np.broadcast_to(-s[:, :, None], s.shape + (LANES,))
    c, s = _cs(f2[:, None] * t2[None, :], n2)
    fb = np.block([[c, s], [-s, c]])
    c, s = _cs(t2[:, None] * f2[None, :], n2)
    fbi = np.block([[c, -s], [s, c]])
    t1o = np.arange(h1) + h1 // 2
    k = t1o[:, None, None, None] * f1[None, None, :, None] * n2 + lo[None, None, None, :] * f1[None, None, :, None]
    c, s = _cs(k, n)
    c = c * eye[None, :, None, :] * mirror_weight[None, None, :, None]
    s = s * eye[None, :, None, :] * mirror_weight[None, None, :, None]
    mc = np.concatenate([c.reshape(h1 * g, nf * g), -s.reshape(h1 * g, nf * g)], axis=1)
    return (jnp.asarray(ma, BF16), jnp.asarray(fb, BF16), jnp.asarray(fbi, BF16), jnp.asarray(mc, BF16),
            jnp.asarray(twc, F32), jnp.asarray(tws, F32))


@functools.lru_cache(maxsize=None)
def _one_stage_tables(seq):
    n = 2 * seq
    f = np.arange(n)
    t = np.arange(seq)
    c, s = _cs(f[:, None] * t[None, :], n)
    fwd = np.concatenate([c, -s], axis=0)
    to = np.arange(seq) + seq // 2
    c, s = _cs(to[:, None] * f[None, :], n)
    inv = np.concatenate([c, -s], axis=1)
    return jnp.asarray(fwd, BF16), jnp.asarray(inv, BF16)


@functools.lru_cache(maxsize=None)
def _filter_constants(seq):
    pos = np.arange(seq, dtype=np.float32)
    t = pos / np.float32(max(seq - 1, 1))
    bands = (FILT_EMB - 1) // 2
    fb = np.linspace(1e-4, bands - 1, bands, dtype=np.float32)
    ang = np.float32(2.0 * math.pi / seq) * pos[:, None] * fb[None, :]
    feats = np.zeros((seq, FILT_HID), np.float32)
    feats[:, 0] = t
    feats[:, 1:1 + bands] = np.cos(ang)
    feats[:, 1 + bands:FILT_EMB] = -np.sin(ang)
    dist = np.abs(pos - (seq // 2)) / np.float32(seq / 2.0)
    dist = np.broadcast_to(dist[:, None], (seq, LANES)).astype(np.float32)
    deltas = np.abs(np.linspace(math.log(HY_TARGET) / HY_SLOW_DECAY, math.log(HY_TARGET) / HY_FAST_DECAY,
                                HY_C, dtype=np.float32))
    deltas = np.tile(deltas, HY_ORDER)[None, :]
    return jnp.asarray(feats), jnp.asarray(dist), jnp.asarray(deltas)


def _filter_hidden_body(feats_ref, w1_ref, b1_ref, w2_ref, b2_ref, fr_ref, o_ref):
    fr = fr_ref[...]
    h = jnp.sin(fr * (_dot_exact(feats_ref[...], w1_ref[...]) + b1_ref[...]))
    o_ref[...] = jnp.sin(fr * (_dot_exact(h, w2_ref[...]) + b2_ref[...]))


def _filter_hidden(seq, fweights):
    w1, b1, w2, b2, _, freq = fweights
    depth = w1.shape[0]
    feats, _, _ = _filter_constants(seq)
    mat = pl.BlockSpec((None, FILT_HID, FILT_HID), lambda l: (l, 0, 0))
    vec = pl.BlockSpec((None, 1, FILT_HID), lambda l: (l, 0, 0))
    return pl.pallas_call(
        _filter_hidden_body,
        grid=(depth,),
        in_specs=[pl.BlockSpec((seq, FILT_HID), lambda l: (0, 0)), mat, vec, mat, vec, vec],
        out_specs=pl.BlockSpec((None, seq, FILT_HID), lambda l: (l, 0, 0)),
        out_shape=jax.ShapeDtypeStruct((depth, seq, FILT_HID), F32),
        compiler_params=_params("arbitrary"),
        name="hyena_filter_hidden",
    )(feats, w1, b1, w2, b2, freq)


def _implicit_filter(hid_ref, dist_ref, delt_ref, w3_ref):
    return _dot_exact(hid_ref[...], w3_ref[...]) * jnp.exp(-dist_ref[...] * delt_ref[...])


def _emit_pipelined(n, *stages):
    carried = {}
    for step in range(n + len(stages) - 1):
        for k, stage in enumerate(stages):
            i = step - k
            if 0 <= i < n:
                carried[i] = stage(i, carried.get(i))
    return carried


def _forward_two_stage(src_ref, a_ref, ma_ref, fb_ref, twc_ref, tws_ref, n1, *spectrum_stages):
    n2 = DFT_INNER
    g = ROW_GROUP
    h1 = n1 // 2
    nf = twc_ref.shape[1]
    cw = src_ref.shape[-1]
    half = nf * g

    def transform(i, _):
        xs = [src_ref[:, hi * g:(hi + 1) * g, :].reshape(h1 * g, cw).astype(BF16)
              for hi in range(i * LANE_PAIR, (i + 1) * LANE_PAIR)]
        return _dot(ma_ref[...], jnp.concatenate(xs, axis=1))

    def twiddle(i, out):
        for j in range(LANE_PAIR):
            hi = i * LANE_PAIR + j
            lanes = slice(j * cw, (j + 1) * cw)
            twc = twc_ref[hi]
            tws = tws_ref[hi]
            for f1 in range(nf):
                c = twc[f1:f1 + 1]
                s = tws[f1:f1 + 1]
                re = out[f1 * g:(f1 + 1) * g, lanes]
                im = out[half + f1 * g:half + (f1 + 1) * g, lanes]
                a_ref[hi, f1 * g:(f1 + 1) * g, :] = re * c - im * s
                a_ref[hi, half + f1 * g:half + (f1 + 1) * g, :] = re * s + im * c

    _emit_pipelined(n2 // g // LANE_PAIR, transform, twiddle)

    def spectrum(i, _):
        zs = []
        for f1 in range(i * LANE_PAIR, (i + 1) * LANE_PAIR):
            ar = a_ref[:, f1 * g:(f1 + 1) * g, :].reshape(n2, cw)
            ai = a_ref[:, half + f1 * g:half + (f1 + 1) * g, :].reshape(n2, cw)
            zs.append(jnp.concatenate([ar, ai], axis=0).astype(BF16))
        return _dot(fb_ref[...], jnp.concatenate(zs, axis=1))

    _emit_pipelined(nf // LANE_PAIR, spectrum, *spectrum_stages)


def _filter_two_stage_body(hid_ref, dist_ref, delt_ref, w3_ref, ma_ref, fb_ref, twc_ref, tws_ref,
                           o_ref, h_ref, a_ref, *, n1):
    h = _implicit_filter(hid_ref, dist_ref, delt_ref, w3_ref)
    h_ref[...] = h.reshape(h_ref.shape)
    inv_n = 1.0 / (n1 * DFT_INNER)
    cw = h_ref.shape[-1]

    def store(i, spec):
        spec = (spec * inv_n).astype(o_ref.dtype)
        for j in range(LANE_PAIR):
            o_ref[i * LANE_PAIR + j] = spec[:, j * cw:(j + 1) * cw]

    _forward_two_stage(h_ref, a_ref, ma_ref, fb_ref, twc_ref, tws_ref, n1, store)


def _filter_one_stage_body(hid_ref, dist_ref, delt_ref, w3_ref, fwd_ref, o_ref):
    h = _implicit_filter(hid_ref, dist_ref, delt_ref, w3_ref)
    inv_n = 1.0 / fwd_ref.shape[0] * 2.0
    o_ref[...] = (_dot(fwd_ref[...], h.astype(BF16)) * inv_n).astype(o_ref.dtype)


def _filter_input_specs(seq, cw):
    return [
        pl.BlockSpec((None, seq, FILT_HID), lambda l, j: (l, 0, 0)),
        pl.BlockSpec((seq, LANES), lambda l, j: (0, 0)),
        pl.BlockSpec((1, cw), lambda l, j: (0, j)),
        pl.BlockSpec((None, FILT_HID, cw), lambda l, j: (l, 0, j)),
    ]


def _filter_weights(hy_f_w1, hy_f_b1, hy_f_w2, hy_f_b2, hy_f_w3, hy_f_freq):
    depth = hy_f_w1.shape[0]
    w1 = jnp.pad(hy_f_w1, ((0, 0), (0, FILT_HID - FILT_EMB), (0, 0)))
    return (w1, hy_f_b1.reshape(depth, 1, FILT_HID), hy_f_w2, hy_f_b2.reshape(depth, 1, FILT_HID),
            hy_f_w3, hy_f_freq.reshape(depth, 1, FILT_HID))


def _filter_spectra_two_stage(seq, fweights):
    depth = fweights[0].shape[0]
    cw = LANES
    n2 = DFT_INNER
    n1 = 2 * seq // n2
    _, dist, deltas = _filter_constants(seq)
    ma, fb, _, _, twc, tws = _two_stage_tables(seq)
    nf = twc.shape[1]
    ncol = HY_ORDER * HY_C
    return pl.pallas_call(
        functools.partial(_filter_two_stage_body, n1=n1),
        grid=(depth, ncol // cw),
        in_specs=[
            *_filter_input_specs(seq, cw),
            pl.BlockSpec(ma.shape, lambda l, j: (0, 0)),
            pl.BlockSpec(fb.shape, lambda l, j: (0, 0)),
            pl.BlockSpec(twc.shape, lambda l, j: (0, 0, 0)),
            pl.BlockSpec(tws.shape, lambda l, j: (0, 0, 0)),
        ],
        out_specs=pl.BlockSpec((None, nf, 2 * n2, cw), lambda l, j: (l, 0, 0, j)),
        out_shape=jax.ShapeDtypeStruct((depth, nf, 2 * n2, ncol), BF16),
        scratch_shapes=[pltpu.VMEM((n1 // 2, n2, cw), F32),
                        pltpu.VMEM((n2 // ROW_GROUP, 2 * nf * ROW_GROUP, cw), F32)],
        compiler_params=_params("arbitrary", "arbitrary"),
        name="hyena_filter_long",
    )(_filter_hidden(seq, fweights), dist, deltas, fweights[4], ma, fb, twc, tws)


def _filter_spectra_one_stage(seq, fweights):
    depth = fweights[0].shape[0]
    cw = LANES
    _, dist, deltas = _filter_constants(seq)
    fwd, _ = _one_stage_tables(seq)
    ncol = HY_ORDER * HY_C
    return pl.pallas_call(
        _filter_one_stage_body,
        grid=(depth, ncol // cw),
        in_specs=[*_filter_input_specs(seq, cw), pl.BlockSpec(fwd.shape, lambda l, j: (0, 0))],
        out_specs=pl.BlockSpec((None, 4 * seq, cw), lambda l, j: (l, 0, j)),
        out_shape=jax.ShapeDtypeStruct((depth, 4 * seq, ncol), BF16),
        compiler_params=_params("arbitrary", "arbitrary"),
        name="hyena_filter_short",
    )(_filter_hidden(seq, fweights), dist, deltas, fweights[4], fwd)


def _short_conv(x, w_ref, b_ref, part):
    seq = x.shape[0]
    rows = lax.broadcasted_iota(jnp.int32, x.shape, 0)
    prev = jnp.where(rows == 0, 0.0, pltpu.roll(x, 1, 0))
    nxt = jnp.where(rows == seq - 1, 0.0, pltpu.roll(x, seq - 1, 0))
    w = w_ref[part]
    return prev * w[0:1] + x * w[1:2] + nxt * w[2:3] + b_ref[part:part + 1, :]


def _complex_mul(x, h, half):
    xr, xi = x[:half], x[half:]
    hr, hi = h[:half].astype(F32), h[half:].astype(F32)
    return jnp.concatenate([xr * hr - xi * hi, xr * hi + xi * hr], axis=0)


def _hyena_long_body(v_ref, x1_ref, x2_ref, g_ref, cw_ref, cb_ref, skip_ref, hf0_ref, hf1_ref,
                     ma_ref, fb_ref, fbi_ref, mc_ref, twc_ref, tws_ref, o_ref, y_ref, c_ref, a_ref, w_ref,
                     *, n1):
    n2 = DFT_INNER
    g = ROW_GROUP
    h1 = n1 // 2
    seq, cw = v_ref.shape
    nf = twc_ref.shape[1]
    half = nf * g

    def long_conv(hf_ref):
        def filter_and_invert(i, spec):
            hf = jnp.concatenate([hf_ref[i * LANE_PAIR + j] for j in range(LANE_PAIR)], axis=1)
            return _dot(fbi_ref[...], _complex_mul(spec, hf, n2).astype(BF16))

        def store_inverse(i, w):
            for j in range(LANE_PAIR):
                r0 = (i * LANE_PAIR + j) * g
                r1 = half + r0
                w_ref[:, r0:r0 + g, :] = w[:n2, j * cw:(j + 1) * cw].reshape(n2 // g, g, cw)
                w_ref[:, r1:r1 + g, :] = w[n2:, j * cw:(j + 1) * cw].reshape(n2 // g, g, cw)

        _forward_two_stage(y_ref, a_ref, ma_ref, fb_ref, twc_ref, tws_ref, n1, filter_and_invert, store_inverse)

        def untwiddle(i, _):
            cols = []
            for hi in range(i * LANE_PAIR, (i + 1) * LANE_PAIR):
                res, ims = [], []
                twc = twc_ref[hi]
                tws = tws_ref[hi]
                for f1 in range(nf):
                    c = twc[f1:f1 + 1]
                    s = tws[f1:f1 + 1]
                    re = w_ref[hi, f1 * g:(f1 + 1) * g, :]
                    im = w_ref[hi, half + f1 * g:half + (f1 + 1) * g, :]
                    res.append(re * c + im * s)
                    ims.append(im * c - re * s)
                cols.append(jnp.concatenate(res + ims, axis=0).astype(BF16))
            return jnp.concatenate(cols, axis=1)

        def to_time(i, wc):
            y = _dot(mc_ref[...], wc)
            for j in range(LANE_PAIR):
                hi = i * LANE_PAIR + j
                c_ref[:, hi * g:(hi + 1) * g, :] = y[:, j * cw:(j + 1) * cw].reshape(h1, g, cw)

        _emit_pipelined(n2 // g // LANE_PAIR, untwiddle, to_time)

    def load3(ref):
        return ref[...].reshape(seq, cw)

    y_ref[...] = _short_conv(v_ref[...], cw_ref, cb_ref, 0).reshape(y_ref.shape)
    long_conv(hf0_ref)
    y = _short_conv(x1_ref[...], cw_ref, cb_ref, 1) * (load3(c_ref) + skip_ref[0:1, :] * load3(y_ref))
    y_ref[...] = y.reshape(y_ref.shape)
    long_conv(hf1_ref)
    out = _short_conv(x2_ref[...], cw_ref, cb_ref, 2) * (load3(c_ref) + skip_ref[1:2, :] * load3(y_ref))
    o_ref[...] = (out * _silu(g_ref[...])).astype(o_ref.dtype)


def _const_spec(shape):
    nd = len(shape)
    return pl.BlockSpec(shape, lambda *_: (0,) * nd, pipeline_mode=pl.Buffered(1))


def _hyena_long(p, conv_w, conv_b, skip, spectra, layer):
    b, seq, _ = p.shape
    cw = LANES
    n2 = DFT_INNER
    n1 = 2 * seq // n2
    nblk = HY_C // cw
    ma, fb, fbi, mc, twc, tws = _two_stage_tables(seq)
    nf = twc.shape[1]

    def pcol(off):
        return pl.BlockSpec((None, seq, cw), lambda j, i: (i, 0, off // cw + j))

    def hf_spec(order):
        return pl.BlockSpec((None, nf, 2 * n2, cw), lambda j, i: (layer, 0, 0, order * nblk + j),
                            pipeline_mode=pl.Buffered(1))

    stage_rows = (n2 // ROW_GROUP, 2 * nf * ROW_GROUP, cw)
    return pl.pallas_call(
        functools.partial(_hyena_long_body, n1=n1),
        grid=(nblk, b),
        in_specs=[
            pcol(OFF_HYZ), pcol(OFF_HYZ + HY_C), pcol(OFF_HYZ + 2 * HY_C), pcol(OFF_HYG),
            pl.BlockSpec((None, HY_ORDER + 1, SHORT_K, cw), lambda j, i: (layer, 0, 0, j)),
            pl.BlockSpec((None, HY_ORDER + 1, cw), lambda j, i: (layer, 0, j)),
            pl.BlockSpec((None, HY_ORDER, cw), lambda j, i: (layer, 0, j)),
            hf_spec(0), hf_spec(1),
            _const_spec(ma.shape), _const_spec(fb.shape), _const_spec(fbi.shape), _const_spec(mc.shape),
            _const_spec(twc.shape), _const_spec(tws.shape),
        ],
        out_specs=pl.BlockSpec((None, seq, cw), lambda j, i: (i, 0, j)),
        out_shape=jax.ShapeDtypeStruct((b, seq, HY_C), BF16),
        scratch_shapes=[
            pltpu.VMEM((n1 // 2, n2, cw), F32), pltpu.VMEM((n1 // 2, n2, cw), F32),
            pltpu.VMEM(stage_rows, F32), pltpu.VMEM(stage_rows, F32),
        ],
        compiler_params=_params("arbitrary", "arbitrary"),
        name="hyena_long",
    )(p, p, p, p, conv_w, conv_b, skip, spectra, spectra, ma, fb, fbi, mc, twc, tws)


def _hyena_short_body(v_ref, x1_ref, x2_ref, g_ref, cw_ref, cb_ref, skip_ref, hf0_ref, hf1_ref,
                      fwd_ref, inv_ref, o_ref):
    n = fwd_ref.shape[0] // 2

    def long_conv(y, hf_ref):
        spec = _complex_mul(_dot(fwd_ref[...], y.astype(BF16)), hf_ref[...], n)
        return _dot(inv_ref[...], spec.astype(BF16))

    y = _short_conv(v_ref[...], cw_ref, cb_ref, 0)
    y = _short_conv(x1_ref[...], cw_ref, cb_ref, 1) * (long_conv(y, hf0_ref) + skip_ref[0:1, :] * y)
    y = _short_conv(x2_ref[...], cw_ref, cb_ref, 2) * (long_conv(y, hf1_ref) + skip_ref[1:2, :] * y)
    o_ref[...] = (y * _silu(g_ref[...])).astype(o_ref.dtype)


def _hyena_short(p, conv_w, conv_b, skip, spectra, layer):
    b, seq, _ = p.shape
    cw = LANE_PAIR * LANES
    nblk = HY_C // cw
    fwd, inv = _one_stage_tables(seq)

    def pcol(off):
        return pl.BlockSpec((None, seq, cw), lambda j, i: (i, 0, off // cw + j))

    def hf_spec(order):
        return pl.BlockSpec((None, 4 * seq, cw), lambda j, i: (layer, 0, order * nblk + j))

    return pl.pallas_call(
        _hyena_short_body,
        grid=(nblk, b),
        in_specs=[
            pcol(OFF_HYZ), pcol(OFF_HYZ + HY_C), pcol(OFF_HYZ + 2 * HY_C), pcol(OFF_HYG),
            pl.BlockSpec((None, HY_ORDER + 1, SHORT_K, cw), lambda j, i: (layer, 0, 0, j)),
            pl.BlockSpec((None, HY_ORDER + 1, cw), lambda j, i: (layer, 0, j)),
            pl.BlockSpec((None, HY_ORDER, cw), lambda j, i: (layer, 0, j)),
            hf_spec(0), hf_spec(1),
            pl.BlockSpec(fwd.shape, lambda j, i: (0, 0)), pl.BlockSpec(inv.shape, lambda j, i: (0, 0)),
        ],
        out_specs=pl.BlockSpec((None, seq, cw), lambda j, i: (i, 0, j)),
        out_shape=jax.ShapeDtypeStruct((b, seq, HY_C), BF16),
        compiler_params=_params("arbitrary", "arbitrary"),
        name="hyena_short",
    )(p, p, p, p, conv_w, conv_b, skip, spectra, spectra, fwd, inv)


def _chunk_mlp_body(u_ref, v_ref, g_ref, lg_ref, lb_ref, ws_ref, bs_ref, o_ref):
    v = v_ref[...]
    mu = jnp.mean(v, axis=-1, keepdims=True)
    vc = v - mu
    var = jnp.mean(vc * vc, axis=-1, keepdims=True)
    vn = (vc * lax.rsqrt(var + EPS) * lg_ref[...] + lb_ref[...]).astype(BF16)
    rows = v.shape[0]
    for n in range(rows // CHUNK):
        r = slice(n * CHUNK, (n + 1) * CHUNK)
        for h in range(CM_HEADS):
            c = slice(h * CM_HD, (h + 1) * CM_HD)
            s = _dot(ws_ref[h].astype(BF16), vn[r, c]) + bs_ref[:, c]
            o_ref[r, c] = (u_ref[r, c] * s * _silu(g_ref[r, c])).astype(o_ref.dtype)


def _chunk_mlp(p, ln_g, ln_b, w_s, b_s, layer):
    b, seq, _ = p.shape
    bm = min(seq, 512)
    bs_full = jnp.repeat(jnp.swapaxes(b_s, 1, 2), CM_HD, axis=2)

    def pcol(off):
        return pl.BlockSpec((None, bm, CM_C), lambda i, t: (i, t, off // CM_C))

    return pl.pallas_call(
        _chunk_mlp_body,
        grid=(b, seq // bm),
        in_specs=[
            pcol(OFF_CMU), pcol(OFF_CMV), pcol(OFF_CMG),
            pl.BlockSpec((None, 1, CM_C), lambda i, t: (layer, 0, 0)),
            pl.BlockSpec((None, 1, CM_C), lambda i, t: (layer, 0, 0)),
            pl.BlockSpec((None, CM_HEADS, CHUNK, CHUNK), lambda i, t: (layer, 0, 0, 0)),
            pl.BlockSpec((None, CHUNK, CM_C), lambda i, t: (layer, 0, 0)),
        ],
        out_specs=pl.BlockSpec((None, bm, CM_C), lambda i, t: (i, t, 0)),
        out_shape=jax.ShapeDtypeStruct((b, seq, CM_C), BF16),
        compiler_params=_params("arbitrary", "arbitrary"),
        name="chunk_mlp",
    )(p, p, p, ln_g.reshape(-1, 1, CM_C), ln_b.reshape(-1, 1, CM_C), w_s, bs_full)


def _rms_heads(x, g, nheads):
    outs = []
    for h in range(nheads):
        xh = x[:, h * HEAD_DIM:(h + 1) * HEAD_DIM]
        ms = jnp.mean(xh * xh, axis=-1, keepdims=True)
        outs.append(xh * lax.rsqrt(ms + EPS) * g)
    return outs


def _rope(xh, cos, sin):
    lane = lax.broadcasted_iota(jnp.int32, xh.shape, 1)
    quarter = ROPE_AXIS // 2
    partner = jnp.where((lane % ROPE_AXIS) < quarter,
                        pltpu.roll(xh, HEAD_DIM - quarter, 1), pltpu.roll(xh, quarter, 1))
    return xh * cos + partner * sin


def _kv_prep_body(k_ref, v_ref, gk_ref, cos_ref, sin_ref, kn_ref, kr_ref, vt_ref, *, rotary):
    heads = _rms_heads(k_ref[...], gk_ref[...], N_KV_HEADS)
    for h, kh in enumerate(heads):
        c = slice(h * HEAD_DIM, (h + 1) * HEAD_DIM)
        kn_ref[:, c] = kh
        if rotary:
            kh = _rope(kh, cos_ref[...], sin_ref[...])
        kr_ref[:, c] = kh.astype(BF16)
        vt_ref[h, 0:HEAD_DIM, :] = v_ref[:, c].T.astype(BF16)
        vt_ref[h, HEAD_DIM:, :] = jnp.ones((ATT_ONES_ROWS, v_ref.shape[0]), BF16)


def _kv_prep(p, k_norm_g, cos, sin, layer, rotary):
    b, seq, _ = p.shape
    bm = min(seq, 512)
    blk = pl.BlockSpec((None, bm, KV_C), lambda i, t: (i, t, 0))
    vt_blk = pl.BlockSpec((None, N_KV_HEADS, HEAD_DIM + ATT_ONES_ROWS, bm), lambda i, t: (i, 0, 0, t))
    return pl.pallas_call(
        functools.partial(_kv_prep_body, rotary=rotary),
        grid=(b, seq // bm),
        in_specs=[
            pl.BlockSpec((None, bm, KV_C), lambda i, t: (i, t, OFF_K // KV_C)),
            pl.BlockSpec((None, bm, KV_C), lambda i, t: (i, t, OFF_V // KV_C)),
            pl.BlockSpec((None, 1, HEAD_DIM), lambda i, t: (layer, 0, 0)),
            pl.BlockSpec((bm, HEAD_DIM), lambda i, t: (t, 0)),
            pl.BlockSpec((bm, HEAD_DIM), lambda i, t: (t, 0)),
        ],
        out_specs=[blk, blk, vt_blk],
        out_shape=[jax.ShapeDtypeStruct((b, seq, KV_C), F32),
                   jax.ShapeDtypeStruct((b, seq, KV_C), BF16),
                   jax.ShapeDtypeStruct((b, N_KV_HEADS, HEAD_DIM + ATT_ONES_ROWS, seq), BF16)],
        compiler_params=_params("arbitrary", "arbitrary"),
        name="kv_prep",
    )(p, p, k_norm_g.reshape(-1, 1, HEAD_DIM), cos, sin)


def _attention_body(*refs, rotary, has_ctx):
    q_ref, g_ref, gq_ref, cos_ref, sin_ref, k_ref, vt_ref = refs[:7]
    if has_ctx:
        ck_ref, cv_ref, o_ref, s_ref, p_ref = refs[7:]
    else:
        o_ref, s_ref, p_ref = refs[7:]
    nslot, s_all, nq = s_ref.shape
    tq = nq // GQA
    nsub = q_ref.shape[0] // tq
    s_self = k_ref.shape[0]
    chunk = min(ATT_KEY_CHUNK, s_self)
    sub = ROW_GROUP
    scale = math.log2(math.e) / math.sqrt(HEAD_DIM)
    contract_last = (((1,), (1,)), ((), ()))
    key_blocks = [slice(j * chunk, (j + 1) * chunk) for j in range(s_self // chunk)]
    if has_ctx:
        ck = ck_ref[...].astype(BF16)
        cvt = jnp.concatenate([cv_ref[...].T.astype(BF16),
                               jnp.ones((ATT_ONES_ROWS, cv_ref.shape[0]), BF16)], axis=0)

    def col_reduce(x, op):
        return op(x.reshape(x.shape[0] // sub, sub, nq), axis=0)

    blocks = key_blocks + ([slice(s_self, s_all)] if has_ctx else [])

    def scores(t):
        rows_q = slice(t * tq, (t + 1) * tq)
        heads = _rms_heads(q_ref[rows_q, :], gq_ref[...], GQA)
        if rotary:
            heads = [_rope(qh, cos_ref[rows_q, :], sin_ref[rows_q, :]) for qh in heads]
        q = jnp.concatenate([(qh * scale).astype(BF16) for qh in heads], axis=0)
        slot = t % nslot
        m8 = None
        for i, rows in enumerate(blocks):
            kb = ck if (has_ctx and i == len(blocks) - 1) else k_ref[rows, :]
            st = lax.dot_general(kb, q, contract_last, preferred_element_type=F32)
            s_ref[slot, rows, :] = st
            part = col_reduce(st, jnp.max)
            m8 = part if m8 is None else jnp.maximum(m8, part)
            yield None if i < len(blocks) - 1 else jnp.max(m8, axis=0, keepdims=True)

    def weights(t, m):
        slot = t % nslot
        for rows in blocks:
            p_ref[slot, rows, :] = jnp.exp2(s_ref[slot, rows, :] - m).astype(BF16)
            yield None

    def values(t):
        slot = t % nslot
        rows_q = slice(t * tq, (t + 1) * tq)
        ot = None
        for i, rows in enumerate(blocks):
            vtb = cvt if (has_ctx and i == len(blocks) - 1) else vt_ref[:, rows]
            part = _dot(vtb, p_ref[slot, rows, :])
            ot = part if ot is None else ot + part
            if i == len(blocks) - 1:
                ot = ot[0:HEAD_DIM] / ot[HEAD_DIM:HEAD_DIM + 1]
                for h in range(GQA):
                    c = slice(h * HEAD_DIM, (h + 1) * HEAD_DIM)
                    oh = ot[:, h * tq:(h + 1) * tq].T
                    o_ref[rows_q, c] = (oh * _silu(g_ref[rows_q, c])).astype(o_ref.dtype)
            yield None

    col_max = {}
    for phase in range(nsub + 2):
        stages = []
        if phase < nsub:
            stages.append(("scores", scores(phase)))
        if 0 <= phase - 1 < nsub:
            stages.append(("weights", weights(phase - 1, col_max[phase - 1])))
        if 0 <= phase - 2 < nsub:
            stages.append(("values", values(phase - 2)))
        for _ in blocks:
            for name, steps in stages:
                out = next(steps)
                if name == "scores":
                    col_max[phase] = out


def _attention(p, q_norm_g, cos, sin, keys, values_t, ctx_k, ctx_v, layer, rotary, tq):
    b, seq, _ = p.shape
    skv = keys.shape[1]
    gw = GQA * HEAD_DIM
    has_ctx = ctx_k is not None
    in_specs = [
        pl.BlockSpec((None, tq, gw), lambda i, h, t: (i, t, OFF_Q // gw + h)),
        pl.BlockSpec((None, tq, gw), lambda i, h, t: (i, t, OFF_ATG // gw + h)),
        pl.BlockSpec((None, 1, HEAD_DIM), lambda i, h, t: (layer, 0, 0)),
        pl.BlockSpec((tq, HEAD_DIM), lambda i, h, t: (t, 0)),
        pl.BlockSpec((tq, HEAD_DIM), lambda i, h, t: (t, 0)),
        pl.BlockSpec((None, skv, HEAD_DIM), lambda i, h, t: (i, 0, h)),
        pl.BlockSpec((None, None, HEAD_DIM + ATT_ONES_ROWS, skv), lambda i, h, t: (i, h, 0, 0)),
    ]
    args = [p, p, q_norm_g.reshape(-1, 1, HEAD_DIM), cos, sin, keys, values_t]
    s_all = skv
    if has_ctx:
        past = ctx_k.shape[2]
        spec = pl.BlockSpec((None, None, past, HEAD_DIM), lambda i, h, t: (i, layer, 0, h))
        in_specs += [spec, spec]
        args += [ctx_k, ctx_v]
        s_all += past
    slots = (ATT_PIPE_SLOTS, s_all, GQA * ATT_SUBTILE_ROWS)
    return pl.pallas_call(
        functools.partial(_attention_body, rotary=rotary, has_ctx=has_ctx),
        grid=(b, N_KV_HEADS, seq // tq),
        in_specs=in_specs,
        out_specs=pl.BlockSpec((None, tq, gw), lambda i, h, t: (i, t, h)),
        out_shape=jax.ShapeDtypeStruct((b, seq, ATT_C), BF16),
        scratch_shapes=[pltpu.VMEM(slots, F32), pltpu.VMEM(slots, BF16)],
        compiler_params=_params("arbitrary", "arbitrary", "arbitrary"),
        name="attention",
    )(*args)


def _out_proj_body(yh_ref, yc_ref, ya_ref, x_ref, mod_ref, w_ref, lg_ref, lb_ref, o_ref):
    d = x_ref.shape[-1]
    gate = mod_ref[:, 2 * d:3 * d]
    mixed = jnp.concatenate([yh_ref[...], yc_ref[...], ya_ref[...]], axis=1)
    r = DN_ALPHA * x_ref[...] + gate * _dot(mixed, w_ref[...])
    mu = jnp.mean(r, axis=-1, keepdims=True)
    rc = r - mu
    var = jnp.mean(rc * rc, axis=-1, keepdims=True)
    o_ref[...] = rc * lax.rsqrt(var + EPS) * lg_ref[...] + lb_ref[...]


def _out_proj(y_hy, y_cm, y_at, x, mod, row0, w_out_bf, ln_g, ln_b, layer):
    g, t, d = x.shape
    bm = 512

    def rows(c):
        return pl.BlockSpec((None, bm, c), lambda b, i: (b, i, 0))

    return pl.pallas_call(
        _out_proj_body,
        grid=(g, t // bm),
        in_specs=[
            rows(HY_C), rows(CM_C), rows(ATT_C), rows(d),
            pl.BlockSpec((None, None, 1, 3 * d), lambda b, i: (layer, row0 + b, 0, 0)),
            pl.BlockSpec((None, D_MIX, d), lambda b, i: (layer, 0, 0)),
            pl.BlockSpec((None, 1, d), lambda b, i: (layer, 0, 0)),
            pl.BlockSpec((None, 1, d), lambda b, i: (layer, 0, 0)),
        ],
        out_specs=rows(d),
        out_shape=jax.ShapeDtypeStruct((g, t, d), F32),
        compiler_params=_params("arbitrary", "arbitrary"),
        name="out_proj_ln",
    )(y_hy, y_cm, y_at, x, mod, w_out_bf, ln_g.reshape(-1, 1, d), ln_b.reshape(-1, 1, d))


@functools.lru_cache(maxsize=None)
def _rope_tables(seq):
    rows = seq // GRID_W
    row = np.repeat(np.arange(rows, dtype=np.float32), GRID_W)
    col = np.tile(np.arange(GRID_W, dtype=np.float32), rows)
    inv = (np.float32(ROPE_THETA) ** (-np.arange(0, ROPE_AXIS, 2, dtype=np.float32) / np.float32(ROPE_AXIS))
           ).astype(np.float32)
    ra = row[:, None] * inv[None, :]
    ca = col[:, None] * inv[None, :]
    cos = np.concatenate([np.cos(ra), np.cos(ra), np.cos(ca), np.cos(ca)], axis=1)
    sin = np.concatenate([-np.sin(ra), np.sin(ra), -np.sin(ca), np.sin(ca)], axis=1)
    return jnp.asarray(cos, F32), jnp.asarray(sin, F32)


def _mixer_layer(x, seq, mod, row0, layer, rotary, w_in_bf, w_out_bf, spectra, hy_conv_w, hy_conv_b, hy_skip,
                 cm_ln_g, cm_ln_b, cm_w_s, cm_b_s, q_norm_g, k_norm_g, ln_g, ln_b, ctx_k, ctx_v, cos, sin):
    g, t, d = x.shape
    nreq = g * t // seq
    p = _in_proj(x, mod, row0, w_in_bf, layer).reshape(nreq, seq, D_IN)
    if rotary:
        y_hy = _hyena_long(p, hy_conv_w, hy_conv_b, hy_skip, spectra, layer)
    else:
        y_hy = _hyena_short(p, hy_conv_w, hy_conv_b, hy_skip, spectra, layer)
    y_cm = _chunk_mlp(p, cm_ln_g, cm_ln_b, cm_w_s, cm_b_s, layer)
    k_normed, keys, values = _kv_prep(p, k_norm_g, cos, sin, layer, rotary)
    y_at = _attention(p, q_norm_g, cos, sin, keys, values, ctx_k, ctx_v, layer, rotary,
                      tq=min(seq, ATT_SUBTILES * ATT_SUBTILE_ROWS))
    x_new = _out_proj(y_hy.reshape(g, t, HY_C), y_cm.reshape(g, t, CM_C), y_at.reshape(g, t, ATT_C),
                      x, mod, row0, w_out_bf, ln_g, ln_b, layer)
    return x_new, k_normed, p


def kernel(x_prompt, x_sample, cache_k, cache_v, c, c_ctx, w_mod, b_mod, w_in, hy_conv_w, hy_conv_b, hy_f_w1, hy_f_b1, hy_f_w2, hy_f_b2, hy_f_w3, hy_f_freq, hy_skip, cm_ln_g, cm_ln_b, cm_w_s, cm_b_s, q_norm_g, k_norm_g, w_out, ln_g, ln_b):
    batch, seq, d = x_prompt.shape
    dec_batch, dec_seq, _ = x_sample.shape
    depth = w_in.shape[0]
    past = cache_k.shape[2]

    cond = jnp.concatenate([c_ctx[None, :], c, jnp.zeros((MOD_ROWS - 1 - dec_batch, d), F32)], axis=0)
    mod = _modulation(cond, w_mod, b_mod).reshape(depth, MOD_ROWS, 1, 3 * d)

    w_in_bf = w_in.astype(BF16)
    w_out_bf = w_out.astype(BF16)
    fweights = _filter_weights(hy_f_w1, hy_f_b1, hy_f_w2, hy_f_b2, hy_f_w3, hy_f_freq)
    spectra_ctx = _filter_spectra_one_stage(seq, fweights)
    spectra_lat = _filter_spectra_two_stage(dec_seq, fweights)
    conv_w = hy_conv_w.reshape(depth, SHORT_K, HY_ORDER + 1, HY_C).transpose(0, 2, 1, 3)
    conv_b = hy_conv_b.reshape(depth, HY_ORDER + 1, HY_C)
    cos, sin = _rope_tables(dec_seq)
    ctx_k = cache_k.reshape(dec_batch, depth, past, KV_C)
    ctx_v = cache_v.reshape(dec_batch, depth, past, KV_C)

    xp = x_prompt.reshape(1, batch * seq, d)
    xs = x_sample
    new_k, new_v = [], []
    for layer in range(depth):
        common = dict(layer=layer, w_in_bf=w_in_bf, w_out_bf=w_out_bf, hy_conv_w=conv_w, hy_conv_b=conv_b,
                      hy_skip=hy_skip, cm_ln_g=cm_ln_g, cm_ln_b=cm_ln_b, cm_w_s=cm_w_s, cm_b_s=cm_b_s,
                      q_norm_g=q_norm_g, k_norm_g=k_norm_g, ln_g=ln_g, ln_b=ln_b, cos=cos, sin=sin)
        xp, k_l, p_ctx = _mixer_layer(xp, seq, mod, 0, rotary=False, spectra=spectra_ctx,
                                      ctx_k=None, ctx_v=None, **common)
        new_k.append(k_l.reshape(batch, seq, N_KV_HEADS, HEAD_DIM))
        new_v.append(p_ctx[:, :, OFF_V:OFF_V + KV_C].reshape(batch, seq, N_KV_HEADS, HEAD_DIM))
        xs, _, _ = _mixer_layer(xs, dec_seq, mod, 1, rotary=True, spectra=spectra_lat,
                                ctx_k=ctx_k, ctx_v=ctx_v, **common)
    return (xp.reshape(batch, seq, d), xs, jnp.stack(new_k, axis=1), jnp.stack(new_v, axis=1))
```
